```python
import math
import jax
import jax.numpy as jnp
from jax import lax
import numpy as np

D_MODEL = 2048
BATCH = 4
SEQ = 2048
DEPTH = 4
DEC_BATCH = 128
DEC_SEQ = 8
PAST_LEN = 8192
PAGE_SIZE = 128

HEAD_DIM = 64
MOBA_HEADS = 8
MOBA_KV_HEADS = 2
MOBA_BLOCK = 256
MOBA_TOPK = 3
NSA_HEADS = 8
NSA_CMP_LEN = 32
NSA_CMP_STRIDE = 16
NSA_CMP_HIDDEN = 128
NSA_SEL_BLOCK = 64
NSA_TOPN = 16
NSA_WINDOW = 512
MLA_HEADS = 8
MLA_Q_LORA = 384
MLA_KV_LORA = 128
MLA_NOPE = 64
MLA_ROPE = 32
MLA_V = 64
ROPE_BASE = 10000.0
N_BRANCH = 3
BRANCH_WIDTH = 512
N_EXPERTS = 16
N_GROUPS = 4
TOPK_EXPERTS = 2
D_EXPERT = 512
PLE_DIM = 256
DN_ALPHA = (2 * DEPTH) ** 0.25
DN_BETA = (8 * DEPTH) ** -0.25
LN_EPS = 1e-5
RMS_EPS = 1e-6
NEG_INF = -1e30
FORCE_BONUS = 1e4
MOBA_TOKENS_PER_STEP = 32
NSA_TOKENS_PER_STEP = 256
QUERY_BLOCK = 128
IN_SPLITS = (
    MOBA_HEADS * HEAD_DIM,
    MOBA_KV_HEADS * 2 * HEAD_DIM,
    NSA_HEADS * HEAD_DIM,
    6 * HEAD_DIM,
    3 * NSA_HEADS,
    MLA_Q_LORA,
    MLA_KV_LORA + MLA_ROPE,
    N_BRANCH * D_MODEL,
)
IN_COLS = sum(IN_SPLITS)

kernel_name = 'hybrid_moba_nsa_mla_moe_decoder_step'


def layer_norm(x, g, b):
    xf = x.astype(jnp.float32)
    mu = xf.mean(-1, keepdims=True)
    var = jnp.square(xf - mu).mean(-1, keepdims=True)
    return ((xf - mu) * lax.rsqrt(var + LN_EPS) * g.astype(jnp.float32) + b.astype(jnp.float32)).astype(x.dtype)


def rms_norm(x, g):
    xf = x.astype(jnp.float32)
    return (xf * lax.rsqrt(jnp.square(xf).mean(-1, keepdims=True) + RMS_EPS) * g.astype(jnp.float32)).astype(x.dtype)


def masked_softmax(s, mask):
    s = jnp.where(mask, s.astype(jnp.float32), NEG_INF)
    e = jnp.exp(s - s.max(-1, keepdims=True)) * mask
    return e / jnp.maximum(e.sum(-1, keepdims=True), 1e-30)


def alibi_slopes(n):
    return jnp.asarray(2.0 ** (-8.0 * np.arange(1, n + 1) / n), dtype=jnp.float32)


def rope(x, pos):
    half = x.shape[-1] // 2
    freqs = ROPE_BASE ** (-jnp.arange(half, dtype=jnp.float32) / half)
    ang = pos.astype(jnp.float32)[:, None] * freqs[None, :]
    shape = (pos.shape[0],) + (1,) * (x.ndim - 3) + (half,)
    cos = jnp.cos(ang).reshape(shape)
    sin = jnp.sin(ang).reshape(shape)
    xf = x.astype(jnp.float32)
    x1, x2 = xf[..., :half], xf[..., half:]
    return jnp.concatenate([x1 * cos - x2 * sin, x1 * sin + x2 * cos], -1).astype(x.dtype)


def make_fetch(new_rows, paged):
    T = new_rows.shape[1]

    def fetch(bb, pos, *idx):
        if paged is None:
            return new_rows[(bb, jnp.clip(pos, 0, T - 1)) + idx]
        pool, layer, table = paged
        page = pool.shape[2]
        past_len = table.shape[1] * page
        new = new_rows[(bb, jnp.clip(pos - past_len, 0, T - 1)) + idx]
        pp = jnp.clip(pos, 0, past_len - 1)
        old = pool[(layer, table[bb, pp // page], pp % page) + idx]
        is_old = pos < past_len
        is_old = is_old.reshape(is_old.shape + (1,) * (new.ndim - is_old.ndim))
        return jnp.where(is_old, old.astype(new.dtype), new)

    return fetch


def full_rows(new_rows, paged, col):
    cur = new_rows[(slice(None), slice(None)) + col]
    if paged is None:
        return cur
    pool, layer, table = paged
    old = pool[(layer, table, slice(None)) + col]
    old = old.reshape((table.shape[0], -1) + old.shape[3:])
    return jnp.concatenate([old.astype(cur.dtype), cur], axis=1)


def map_tokens(fn, xs, per_step):
    n = xs[0].shape[0]
    c = math.gcd(n, per_step)
    out = lax.map(fn, tuple(a.reshape((n // c, c) + a.shape[1:]) for a in xs))
    return out.reshape((n,) + out.shape[2:])


def moba_attention(q, k_full, fetch_kv, q_pos):
    B, Q, H, HD = q.shape
    L, KVH = k_full.shape[1], k_full.shape[2]
    G = H // KVH
    nb = -(-L // MOBA_BLOCK)
    k_blocks = jnp.pad(k_full, ((0, 0), (0, nb * MOBA_BLOCK - L), (0, 0), (0, 0))).reshape(B, nb, MOBA_BLOCK, KVH, HD)
    k_mean = k_blocks.astype(jnp.float32).mean(axis=2)
    qg = q.reshape(B, Q, KVH, G, HD)
    gate = jnp.einsum('bqkgd,bnkd->bqkgn', qg.astype(jnp.float32), k_mean)
    own = q_pos // MOBA_BLOCK
    past_blk = jnp.arange(nb)[None, :] < own[:, None]
    gate = jnp.where(past_blk[None, :, None, None, :], gate, NEG_INF)
    _, top = lax.top_k(gate, min(MOBA_TOPK, nb))
    own_b = jnp.broadcast_to(own[None, :, None, None, None], (B, Q, KVH, G, 1))
    blocks = jnp.concatenate([top, own_b], -1)
    blk_ok = jnp.concatenate([top < own_b, jnp.ones_like(own_b, dtype=bool)], -1)
    slopes = alibi_slopes(H).reshape(KVH, G)
    scale = HD ** -0.5
    kvh = jnp.arange(KVH)[None, :, None, None, None]
    offs = jnp.arange(MOBA_BLOCK)

    def step(args):
        qc, blk, ok, bb, tp = args
        kpos = blk[..., None] * MOBA_BLOCK + offs
        b5 = bb[:, None, None, None, None]
        kk = k_full[b5, jnp.clip(kpos, 0, L - 1), kvh]
        vv = fetch_kv(b5, kpos, kvh, 1)
        dist = tp[:, None, None, None, None] - kpos
        s = jnp.einsum('ckgd,ckgnjd->ckgnj', qc, kk).astype(jnp.float32) * scale - slopes[None, :, :, None, None] * dist
        mask = ok[..., None] & (dist >= 0)
        c = s.shape[0]
        p = masked_softmax(s.reshape(c, KVH, G, -1), mask.reshape(c, KVH, G, -1)).reshape(s.shape)
        return jnp.einsum('ckgnj,ckgnjd->ckgd', p.astype(vv.dtype), vv)

    N = B * Q
    out = map_tokens(step, (qg.reshape(N, KVH, G, HD), blocks.reshape(N, KVH, G, -1), blk_ok.reshape(N, KVH, G, -1),
                            jnp.repeat(jnp.arange(B), Q), jnp.tile(q_pos, B)), MOBA_TOKENS_PER_STEP)
    return out.reshape(B, Q, H * HD)


def window_attention(q, wk, wv, q_pos, k_pos0, slopes):
    B, Q, H, HD = q.shape
    Lk = wk.shape[1]
    qb = math.gcd(Q, QUERY_BLOCK)
    nqb = Q // qb
    span = qb + NSA_WINDOW - 1
    first = q_pos[::qb] - k_pos0 - (NSA_WINDOW - 1)
    kidx = first[:, None] + jnp.arange(span)[None, :]
    inside = (kidx >= 0) & (kidx < Lk)
    kc = jnp.clip(kidx, 0, Lk - 1)
    kb, vb = wk[:, kc], wv[:, kc]
    dist = q_pos.reshape(nqb, qb)[:, :, None] - (kidx + k_pos0)[:, None, :]
    mask = (dist >= 0) & (dist < NSA_WINDOW) & inside[:, None, :]
    s = jnp.einsum('bnqhd,bnkd->bnhqk', q.reshape(B, nqb, qb, H, HD), kb).astype(jnp.float32) * HD ** -0.5
    s = s - slopes[None, :, None, None] * dist[:, None]
    p = masked_softmax(s, mask[None, :, None])
    return jnp.einsum('bnhqk,bnkd->bnqhd', p.astype(vb.dtype), vb).reshape(B, Q, H, HD)


def nsa_attention(q, gate_logits, cmp_full, fetch_sel, win_k, win_v, win_pos0, q_pos, lw):
    B, Q, H, HD = q.shape
    L = cmp_full.shape[1]
    slopes = alibi_slopes(H)
    scale = HD ** -0.5
    halves = NSA_CMP_LEN // NSA_CMP_STRIDE
    m = L // NSA_CMP_STRIDE
    n_cmp = m - halves + 1
    sub = cmp_full[:, :m * NSA_CMP_STRIDE].reshape(B, m, NSA_CMP_STRIDE, 2, HD)
    w1 = lw['nsa_cmp_w1'].reshape(2, halves, NSA_CMP_STRIDE, HD, NSA_CMP_HIDDEN)
    pe = lw['nsa_cmp_pos'].reshape(2, halves, NSA_CMP_STRIDE, HD)
    part = jnp.einsum('bmjkd,khjdf->bmkhf', sub, w1) + jnp.einsum('khjd,khjdf->khf', pe, w1)
    pre = sum(part[:, h:h + n_cmp, :, h] for h in range(halves))
    comp = jnp.einsum('bnkf,kfd->bnkd', jax.nn.gelu(pre), lw['nsa_cmp_w2'])
    cst = jnp.arange(n_cmp) * NSA_CMP_STRIDE
    dist_c = q_pos[:, None] - (cst + NSA_CMP_LEN - 1)[None, :]
    s_c = jnp.einsum('bqhd,bnd->bhqn', q, comp[:, :, 0]).astype(jnp.float32) * scale
    s_c = s_c - slopes[:, None, None] * dist_c[None]
    p_cmp = masked_softmax(s_c, (dist_c >= 0)[None, None])
    o_cmp = jnp.einsum('bhqn,bnd->bqhd', p_cmp.astype(comp.dtype), comp[:, :, 1])
    n_sel = -(-L // NSA_SEL_BLOCK)
    sst = jnp.arange(n_sel) * NSA_SEL_BLOCK
    ov = jnp.clip(jnp.minimum(cst[:, None] + NSA_CMP_LEN, sst[None, :] + NSA_SEL_BLOCK) - jnp.maximum(cst[:, None], sst[None, :]), 0, None)
    imp = jnp.einsum('bhqn,nj->bqj', p_cmp, ov.astype(jnp.float32) / NSA_CMP_LEN)
    own = q_pos // NSA_SEL_BLOCK
    j = jnp.arange(n_sel)[None, :]
    forced = (j == 0) | (j == own[:, None]) | (j == own[:, None] - 1)
    valid = j <= own[:, None]
    score = jnp.where(valid[None], imp + jnp.where(forced[None], FORCE_BONUS, 0.0), NEG_INF)
    _, top = lax.top_k(score, min(NSA_TOPN, n_sel))
    top_ok = top <= own[None, :, None]
    offs = jnp.arange(NSA_SEL_BLOCK)

    def step(args):
        qc, blk, ok, bb, tp = args
        kpos = blk[..., None] * NSA_SEL_BLOCK + offs
        rows = fetch_sel(bb[:, None, None], kpos, slice(2, 4))
        dist = tp[:, None, None] - kpos
        s = jnp.einsum('chd,cnjd->chnj', qc, rows[..., 0, :]).astype(jnp.float32) * scale - slopes[None, :, None, None] * dist[:, None]
        mask = (ok[..., None] & (dist >= 0))[:, None]
        c = s.shape[0]
        p = masked_softmax(s.reshape(c, H, -1), mask.reshape(c, 1, -1)).reshape(s.shape)
        return jnp.einsum('chnj,cnjd->chd', p.astype(rows.dtype), rows[..., 1, :])

    N = B * Q
    o_sel = map_tokens(step, (q.reshape(N, H, HD), top.reshape(N, -1), top_ok.reshape(N, -1),
                              jnp.repeat(jnp.arange(B), Q), jnp.tile(q_pos, B)), NSA_TOKENS_PER_STEP).reshape(B, Q, H, HD)
    o_win = window_attention(q, win_k, win_v, q_pos, win_pos0, slopes)
    g = jax.nn.sigmoid(gate_logits.astype(jnp.float32)).astype(q.dtype)[..., None]
    return (g[:, :, 0] * o_cmp + g[:, :, 1] * o_sel + g[:, :, 2] * o_win).reshape(B, Q, H * HD)


def mla_attention(q_lat, q_rope, lat_full, q_pos):
    B, Q, H, C = q_lat.shape
    L = lat_full.shape[1]
    ckv, kr = lat_full[..., :MLA_KV_LORA], lat_full[..., MLA_KV_LORA:]
    scale = (MLA_NOPE + MLA_ROPE) ** -0.5
    qb = math.gcd(Q, QUERY_BLOCK)
    nqb = Q // qb
    k_pos = jnp.arange(L)

    def block(args):
        ql, qr, qp = args
        s = (jnp.einsum('bqhc,blc->bhql', ql, ckv) + jnp.einsum('bqhr,blr->bhql', qr, kr)).astype(jnp.float32) * scale
        p = masked_softmax(s, (k_pos[None, :] <= qp[:, None])[None, None])
        return jnp.einsum('bhql,blc->bqhc', p.astype(ckv.dtype), ckv)

    xs = (q_lat.reshape(B, nqb, qb, H, C).swapaxes(0, 1), q_rope.reshape(B, nqb, qb, H, -1).swapaxes(0, 1), q_pos.reshape(nqb, qb))
    return lax.map(block, xs).swapaxes(0, 1).reshape(B, Q, H, C)


def token_mixer(x, pos, past, lw):
    B, T, _ = x.shape
    mq, mkv, nq, nkv, ngate, qa, kva, mgate = jnp.split(x @ lw['w_in'], np.cumsum(IN_SPLITS)[:-1].tolist(), axis=-1)
    moba_rows = mkv.reshape(B, T, MOBA_KV_HEADS, 2, HEAD_DIM)
    nsa_kv = nkv.reshape(B, T, 6, HEAD_DIM)
    nsa_rows, win_rows = nsa_kv[:, :, :4], nsa_kv[:, :, 4:]
    mla_rows = jnp.concatenate([rms_norm(kva[..., :MLA_KV_LORA], lw['mla_kv_norm']), rope(kva[..., MLA_KV_LORA:], pos)], -1)
    q_mla = (rms_norm(qa, lw['mla_q_norm']) @ lw['mla_w_qb']).reshape(B, T, MLA_HEADS, MLA_NOPE + MLA_ROPE)
    q_lat = jnp.einsum('bthn,chn->bthc', q_mla[..., :MLA_NOPE], lw['mla_w_uk'])
    q_rope = rope(q_mla[..., MLA_NOPE:], pos)
    if past is None:
        pm = pn = pl = None
        win_keys, win_pos0 = win_rows, 0
        new_win = win_rows[:, -min(NSA_WINDOW, T):]
    else:
        pm, pn, pl = past['moba'], past['nsa'], past['mla']
        buf = past['win']
        wb = buf.shape[1]
        win_keys = jnp.concatenate([buf.astype(win_rows.dtype), win_rows], axis=1)
        win_pos0 = PAST_LEN - wb
        new_win = win_keys[:, -wb:]
    o_a = moba_attention(mq.reshape(B, T, MOBA_HEADS, HEAD_DIM), full_rows(moba_rows, pm, (slice(None), 0)),
                         make_fetch(moba_rows, pm), pos)
    o_b = nsa_attention(nq.reshape(B, T, NSA_HEADS, HEAD_DIM), ngate.reshape(B, T, 3, NSA_HEADS),
                        full_rows(nsa_rows, pn, (slice(0, 2),)), make_fetch(nsa_rows, pn),
                        win_keys[:, :, 0], win_keys[:, :, 1], win_pos0, pos, lw)
    o_lat = mla_attention(q_lat, q_rope, full_rows(mla_rows, pl, ()), pos)
    o_c = jnp.einsum('bthc,chv->bthv', o_lat, lw['mla_w_uv']).reshape(B, T, MLA_HEADS * MLA_V)
    up = jnp.einsum('btnr,nrd->btnd', jnp.stack([o_a, o_b, o_c], axis=2), lw['w_branch'])
    gate = jax.nn.sigmoid(mgate.reshape(B, T, N_BRANCH, D_MODEL))
    out = (gate * up).sum(2) @ lw['w_out']
    return out, moba_rows, nsa_rows, mla_rows, new_win


def moe_ffn(h, router_w, router_bias, w_gate, w_up, w_down):
    B, T, D = h.shape
    ht = h.reshape(B * T, D)
    n = ht.shape[0]
    per_group = N_EXPERTS // N_GROUPS
    aff = jax.nn.sigmoid((ht @ router_w).astype(jnp.float32))
    biased = aff + router_bias.astype(jnp.float32)
    group_score = lax.top_k(biased.reshape(n, N_GROUPS, per_group), TOPK_EXPERTS)[0].sum(-1)
    _, group_sel = lax.top_k(group_score, 1)
    in_group = (jnp.arange(N_EXPERTS) // per_group)[None, :] == group_sel
    _, expert_idx = lax.top_k(jnp.where(in_group, biased, NEG_INF), TOPK_EXPERTS)
    w = jnp.take_along_axis(aff, expert_idx, axis=-1)
    w = w / w.sum(-1, keepdims=True)
    combine = (jax.nn.one_hot(expert_idx, N_EXPERTS, dtype=jnp.float32) * w[..., None]).sum(1)
    hidden = jax.nn.silu(jnp.einsum('nd,edf->nef', ht, w_gate)) * jnp.einsum('nd,edf->nef', ht, w_up)
    out = jnp.einsum('nef,efd->nd', hidden * combine[..., None].astype(hidden.dtype), w_down)
    return out.reshape(B, T, D)


def decoder_layer(x, p_i, pos, past, lw, router_w, router_bias):
    mix, moba_r, nsa_r, mla_r, win_s = token_mixer(x, pos, past, lw)
    h = layer_norm(DN_ALPHA * x + mix, lw['ln1_g'], lw['ln1_b'])
    ffn = moe_ffn(h, router_w, router_bias, lw['moe_w_gate'], lw['moe_w_up'], lw['moe_w_down'])
    ple = jax.nn.sigmoid(h @ lw['ple_w_gate']) * (p_i @ lw['ple_w_proj'])
    y = layer_norm(DN_ALPHA * h + ffn + ple, lw['ln2_g'], lw['ln2_b'])
    return y, (moba_r, nsa_r, mla_r, win_s)


def setup_inputs(seed: int = 0) -> dict:
    key = jax.random.key(seed)
    ks = jax.random.split(key, 40)

    def nrm(i, shape, scale):
        return jax.random.normal(ks[i], shape, jnp.float32) * scale

    n_pages = PAST_LEN // PAGE_SIZE
    n_used = DEC_BATCH * n_pages
    n_pool = n_used + n_used // 4
    page_table = jax.random.permutation(ks[0], n_pool)[:n_used].reshape(DEC_BATCH, n_pages).astype(jnp.int32)
    wb = min(NSA_WINDOW, PAST_LEN)
    return {
        'x_prompt': nrm(1, (BATCH, SEQ, D_MODEL), 1.0),
        'x_sample': nrm(2, (DEC_BATCH, DEC_SEQ, D_MODEL), 1.0),
        'cache_moba': nrm(3, (DEPTH, n_pool, PAGE_SIZE, MOBA_KV_HEADS, 2, HEAD_DIM), 1.0),
        'cache_nsa': nrm(4, (DEPTH, n_pool, PAGE_SIZE, 4, HEAD_DIM), 1.0),
        'cache_mla': nrm(5, (DEPTH, n_pool, PAGE_SIZE, MLA_KV_LORA + MLA_ROPE), 1.0),
        'state_nsa_win': nrm(6, (DEPTH, DEC_BATCH, wb, 2, HEAD_DIM), 1.0),
        'page_table': page_table,
        'p_prompt': nrm(7, (DEPTH, BATCH, SEQ, PLE_DIM), 1.0),
        'p_sample': nrm(8, (DEPTH, DEC_BATCH, DEC_SEQ, PLE_DIM), 1.0),
        'w_in': nrm(9, (DEPTH, D_MODEL, IN_COLS), D_MODEL ** -0.5),
        'mla_q_norm': 1.0 + nrm(10, (DEPTH, MLA_Q_LORA), 0.01),
        'mla_w_qb': nrm(11, (DEPTH, MLA_Q_LORA, MLA_HEADS * (MLA_NOPE + MLA_ROPE)), MLA_Q_LORA ** -0.5),
        'mla_kv_norm': 1.0 + nrm(12, (DEPTH, MLA_KV_LORA), 0.01),
        'mla_w_uk': nrm(13, (DEPTH, MLA_KV_LORA, MLA_HEADS, MLA_NOPE), MLA_KV_LORA ** -0.5),
        'mla_w_uv': nrm(14, (DEPTH, MLA_KV_LORA, MLA_HEADS, MLA_V), MLA_KV_LORA ** -0.5),
        'nsa_cmp_pos': nrm(15, (DEPTH, 2, NSA_CMP_LEN, HEAD_DIM), 0.1),
        'nsa_cmp_w1': nrm(16, (DEPTH, 2, NSA_CMP_LEN, HEAD_DIM, NSA_CMP_HIDDEN), (NSA_CMP_LEN * HEAD_DIM) ** -0.5),
        'nsa_cmp_w2': nrm(17, (DEPTH, 2, NSA_CMP_HIDDEN, HEAD_DIM), NSA_CMP_HIDDEN ** -0.5),
        'w_branch': nrm(18, (DEPTH, N_BRANCH, BRANCH_WIDTH, D_MODEL), BRANCH_WIDTH ** -0.5),
        'w_out': nrm(19, (DEPTH, D_MODEL, D_MODEL), D_MODEL ** -0.5 * DN_BETA),
        'ln1_g': 1.0 + nrm(20, (DEPTH, D_MODEL), 0.01),
        'ln1_b': nrm(21, (DEPTH, D_MODEL), 0.01),
        'router_w': nrm(22, (D_MODEL, N_EXPERTS), D_MODEL ** -0.5),
        'router_bias': nrm(23, (N_EXPERTS,), 0.01),
        'moe_w_gate': nrm(24, (DEPTH, N_EXPERTS, D_MODEL, D_EXPERT), D_MODEL ** -0.5),
        'moe_w_up': nrm(25, (DEPTH, N_EXPERTS, D_MODEL, D_EXPERT), D_MODEL ** -0.5),
        'moe_w_down': nrm(26, (DEPTH, N_EXPERTS, D_EXPERT, D_MODEL), D_EXPERT ** -0.5 * DN_BETA),
        'ple_w_proj': nrm(27, (DEPTH, PLE_DIM, D_MODEL), PLE_DIM ** -0.5),
        'ple_w_gate': nrm(28, (DEPTH, D_MODEL, D_MODEL), D_MODEL ** -0.5),
        'ln2_g': 1.0 + nrm(29, (DEPTH, D_MODEL), 0.01),
        'ln2_b': nrm(30, (DEPTH, D_MODEL), 0.01),
    }


def reference(x_prompt, x_sample, cache_moba, cache_nsa, cache_mla, state_nsa_win, page_table, p_prompt, p_sample,
              w_in, mla_q_norm, mla_w_qb, mla_kv_norm, mla_w_uk, mla_w_uv, nsa_cmp_pos, nsa_cmp_w1, nsa_cmp_w2,
              w_branch, w_out, ln1_g, ln1_b, router_w, router_bias, moe_w_gate, moe_w_up, moe_w_down,
              ple_w_proj, ple_w_gate, ln2_g, ln2_b):
    pos_p = jnp.arange(SEQ, dtype=jnp.int32)
    pos_s = PAST_LEN + jnp.arange(DEC_SEQ, dtype=jnp.int32)
    xp, xs = x_prompt, x_sample
    st_p, st_s = [], []
    for i in range(DEPTH):
        lw = {'w_in': w_in[i], 'mla_q_norm': mla_q_norm[i], 'mla_w_qb': mla_w_qb[i], 'mla_kv_norm': mla_kv_norm[i],
              'mla_w_uk': mla_w_uk[i], 'mla_w_uv': mla_w_uv[i], 'nsa_cmp_pos': nsa_cmp_pos[i],
              'nsa_cmp_w1': nsa_cmp_w1[i], 'nsa_cmp_w2': nsa_cmp_w2[i], 'w_branch': w_branch[i], 'w_out': w_out[i],
              'ln1_g': ln1_g[i], 'ln1_b': ln1_b[i], 'moe_w_gate': moe_w_gate[i], 'moe_w_up': moe_w_up[i],
              'moe_w_down': moe_w_down[i], 'ple_w_proj': ple_w_proj[i], 'ple_w_gate': ple_w_gate[i],
              'ln2_g': ln2_g[i], 'ln2_b': ln2_b[i]}
        past = {'moba': (cache_moba, i, page_table), 'nsa': (cache_nsa, i, page_table),
                'mla': (cache_mla, i, page_table), 'win': state_nsa_win[i]}
        xp, sp = decoder_layer(xp, p_prompt[i], pos_p, None, lw, router_w, router_bias)
        xs, ss = decoder_layer(xs, p_sample[i], pos_s, past, lw, router_w, router_bias)
        st_p.append(sp)
        st_s.append(ss)
    moba_p = jnp.stack([s[0] for s in st_p])
    moba_s = jnp.stack([s[0] for s in st_s])
    nsa_p = jnp.stack([s[1] for s in st_p])
    nsa_s = jnp.stack([s[1] for s in st_s])
    mla_p = jnp.stack([s[2] for s in st_p])
    mla_s = jnp.stack([s[2] for s in st_s])
    win_p = jnp.stack([s[3] for s in st_p])
    win_s = jnp.stack([s[3] for s in st_s])
    return (xp, xs, moba_p, moba_s, nsa_p, nsa_s, mla_p, mla_s, win_p, win_s)
```

```python
import functools
import math

import numpy as np
import jax
import jax.numpy as jnp
from jax import lax
from jax.experimental import pallas as pl
from jax.experimental.pallas import tpu as pltpu

F32 = jnp.float32
BF16 = jnp.bfloat16
HIGHEST = lax.Precision.HIGHEST

HEAD_DIM = 64
MOBA_HEADS = 8
MOBA_KV_HEADS = 2
MOBA_BLOCK = 256
MOBA_TOPK = 3
NSA_HEADS = 8
NSA_CMP_LEN = 32
NSA_CMP_STRIDE = 16
NSA_CMP_HIDDEN = 128
NSA_SEL_BLOCK = 64
NSA_TOPN = 16
NSA_WINDOW = 512
MLA_HEADS = 8
MLA_Q_LORA = 384
MLA_KV_LORA = 128
MLA_NOPE = 64
MLA_ROPE = 32
MLA_V = 64
ROPE_BASE = 10000.0
N_BRANCH = 3
BRANCH_WIDTH = 512
N_EXPERTS = 16
N_GROUPS = 4
TOPK_EXPERTS = 2
LN_EPS = 1e-5
RMS_EPS = 1e-6
NEG_INF = -1e30
FORCE_BONUS = 1e4

LANE = 128
VMEM_LIMIT = 52 * 1024 * 1024

C_MQ = 0
C_MKV = 512
C_NQ = 768
C_NKV = 1280
C_NGATE = 1664
C_QA = 1792
C_KVA = 2176
C_MGATE = 2560


def _cparams(sem):
    return pltpu.CompilerParams(dimension_semantics=sem, vmem_limit_bytes=VMEM_LIMIT)


def _round_up(x, m):
    return -(-x // m) * m


def _pick_tile(n, pref):
    t = min(pref, n)
    while n % t:
        t //= 2
    return t


def _mm_kernel(x_ref, w_ref, o_ref, *, precision):
    x = x_ref[...]
    w = w_ref[...]
    if precision is None:
        x = x.astype(BF16)
        w = w.astype(BF16)
    o_ref[...] = jnp.dot(x, w, preferred_element_type=F32, precision=precision).astype(o_ref.dtype)


def mm(x, w, *, tm=512, tn=512, precision=None, out_dtype=F32, name="mm"):
    M, K = x.shape
    N = w.shape[1]
    tm = _pick_tile(M, tm)
    tn = _pick_tile(N, tn)
    return pl.pallas_call(
        functools.partial(_mm_kernel, precision=precision),
        out_shape=jax.ShapeDtypeStruct((M, N), out_dtype),
        grid=(M // tm, N // tn),
        in_specs=[pl.BlockSpec((tm, K), lambda i, j: (i, 0)),
                  pl.BlockSpec((K, tn), lambda i, j: (0, j))],
        out_specs=pl.BlockSpec((tm, tn), lambda i, j: (i, j)),
        compiler_params=_cparams(("parallel", "arbitrary")),
        name=name,
    )(x, w)


def _bmm_kernel(x_ref, w_ref, o_ref):
    o_ref[...] = jnp.dot(x_ref[...], w_ref[...], preferred_element_type=F32, precision=HIGHEST)


def bmm_f32(x, w, name="bmm"):
    B, M, K = x.shape
    N = w.shape[2]
    tm = _pick_tile(M, 1024)
    return pl.pallas_call(
        _bmm_kernel,
        out_shape=jax.ShapeDtypeStruct((B, M, N), F32),
        grid=(B, M // tm),
        in_specs=[pl.BlockSpec((None, tm, K), lambda b, i: (b, i, 0)),
                  pl.BlockSpec((None, K, N), lambda b, i: (b, 0, 0))],
        out_specs=pl.BlockSpec((None, tm, N), lambda b, i: (b, i, 0)),
        compiler_params=_cparams(("parallel", "arbitrary")),
        name=name,
    )(x, w)


def _block_mean_kernel(k_ref, o_ref, *, inv):
    o_ref[...] = jnp.sum(k_ref[...], axis=0, keepdims=True) * inv


def block_mean(k, blk):
    B, L, D = k.shape
    nb = L // blk
    out = pl.pallas_call(
        functools.partial(_block_mean_kernel, inv=1.0 / blk),
        out_shape=jax.ShapeDtypeStruct((B, nb, 1, D), F32),
        grid=(B, nb),
        in_specs=[pl.BlockSpec((None, blk, D), lambda b, n: (b, n, 0))],
        out_specs=pl.BlockSpec((None, None, 1, D), lambda b, n: (b, n, 0, 0)),
        compiler_params=_cparams(("parallel", "arbitrary")),
        name="block_mean",
    )(k)
    return out.reshape(B, nb, D)


def _flash_kernel(*refs, G, slopes, kv_heads, qpos0, kpos0, lk, window, sel_blk, scale, tr, tk, nkt):
    if sel_blk is None:
        q_ref, k_ref, v_ref, o_ref, m_sc, l_sc, acc_sc = refs
        sel_ref = None
    else:
        q_ref, k_ref, v_ref, sel_ref, o_ref, m_sc, l_sc, acc_sc = refs
    b = pl.program_id(0)
    i = pl.program_id(1)
    j = pl.program_id(2)
    shift = int(math.log2(G))

    @pl.when(j == 0)
    def _():
        m_sc[...] = jnp.full_like(m_sc, NEG_INF)
        l_sc[...] = jnp.zeros_like(l_sc)
        acc_sc[...] = jnp.zeros_like(acc_sc)

    q_lo = qpos0 + lax.shift_right_logical(i * tr, shift)
    q_hi = qpos0 + lax.shift_right_logical(i * tr + (tr - 1), shift)
    k_lo = kpos0 + j * tk
    needed = jnp.logical_and(k_lo <= q_hi, j * tk < lk)
    if window is not None:
        needed = jnp.logical_and(needed, k_lo + (tk - 1) > q_lo - window)

    @pl.when(needed)
    def _():
        q = q_ref[...].astype(BF16)
        k = k_ref[...].astype(BF16)
        s = lax.dot_general(q, k, (((1,), (1,)), ((), ())), preferred_element_type=F32) * scale
        rows = i * tr + lax.broadcasted_iota(jnp.int32, (tr, 1), 0)
        qpos = qpos0 + lax.shift_right_logical(rows, shift)
        kidx = j * tk + lax.broadcasted_iota(jnp.int32, (1, tk), 1)
        dist = qpos - (kpos0 + kidx)
        mask = jnp.logical_and(dist >= 0, kidx < lk)
        if window is not None:
            mask = jnp.logical_and(mask, dist < window)
        if slopes is not None:
            g = jnp.bitwise_and(rows, G - 1)
            head = g if kv_heads == 1 else g + G * lax.rem(b, kv_heads)
            slope = jnp.zeros((tr, 1), F32)
            for h, sl in enumerate(slopes):
                slope = jnp.where(head == h, np.float32(sl), slope)
            s = s - slope * dist.astype(F32)
        if sel_ref is not None:
            nbp = sel_ref.shape[-1]
            blk_of_key = lax.shift_right_logical(kidx, int(math.log2(sel_blk)))
            expand = (lax.broadcasted_iota(jnp.int32, (nbp, tk), 0) == blk_of_key).astype(BF16)
            picked = jnp.dot(sel_ref[...].astype(BF16), expand, preferred_element_type=F32)
            mask = jnp.logical_and(mask, picked > 0.5)
        s = jnp.where(mask, s, NEG_INF)
        m_prev = m_sc[...]
        m_new = jnp.maximum(m_prev, jnp.max(s, axis=-1, keepdims=True))
        alpha = jnp.exp(m_prev - m_new)
        p = jnp.where(mask, jnp.exp(s - m_new), 0.0)
        l_sc[...] = alpha * l_sc[...] + jnp.sum(p, axis=-1, keepdims=True)
        acc_sc[...] = alpha * acc_sc[...] + jnp.dot(p.astype(BF16), v_ref[...].astype(BF16),
                                                    preferred_element_type=F32)
        m_sc[...] = m_new

    @pl.when(j == nkt - 1)
    def _():
        o_ref[...] = acc_sc[...] / jnp.maximum(l_sc[...], 1e-30)


def flash(q, k, v, *, G, slopes, kv_heads=1, qpos0, kpos0=0, lk, window=None, sel=None, sel_blk=None,
          scale, tr, tk, name):
    B, R, Dk = q.shape
    Lk, Dv = v.shape[1], v.shape[2]
    tr = _pick_tile(R, tr)
    assert Lk % tk == 0 and G & (G - 1) == 0
    nkt = Lk // tk
    kern = functools.partial(_flash_kernel, G=G, slopes=slopes, kv_heads=kv_heads, qpos0=qpos0, kpos0=kpos0,
                             lk=lk, window=window, sel_blk=sel_blk, scale=scale, tr=tr, tk=tk, nkt=nkt)
    in_specs = [pl.BlockSpec((None, tr, Dk), lambda b, i, j: (b, i, 0)),
                pl.BlockSpec((None, tk, Dk), lambda b, i, j: (b, j, 0)),
                pl.BlockSpec((None, tk, Dv), lambda b, i, j: (b, j, 0))]
    args = [q, k, v]
    if sel is not None:
        in_specs.append(pl.BlockSpec((None, tr, sel.shape[-1]), lambda b, i, j: (b, i, 0)))
        args.append(sel)
    return pl.pallas_call(
        kern,
        out_shape=jax.ShapeDtypeStruct((B, R, Dv), F32),
        grid=(B, R // tr, nkt),
        in_specs=in_specs,
        out_specs=pl.BlockSpec((None, tr, Dv), lambda b, i, j: (b, i, 0)),
        scratch_shapes=[pltpu.VMEM((tr, 1), F32), pltpu.VMEM((tr, 1), F32), pltpu.VMEM((tr, Dv), F32)],
        compiler_params=_cparams(("parallel", "parallel", "arbitrary")),
        name=name,
    )(*args)


def _gelu_proj_kernel(a_ref, b_ref, bias_ref, w_ref, o_ref):
    pre = a_ref[...] + b_ref[...] + bias_ref[...]
    act = jax.nn.gelu(pre)
    o_ref[...] = jnp.dot(act.astype(BF16), w_ref[...].astype(BF16), preferred_element_type=F32)


def gelu_proj(a, b, bias, w):
    M, Fd = a.shape
    D = w.shape[1]
    tm = _pick_tile(M, 1024)
    return pl.pallas_call(
        _gelu_proj_kernel,
        out_shape=jax.ShapeDtypeStruct((M, D), F32),
        grid=(M // tm,),
        in_specs=[pl.BlockSpec((tm, Fd), lambda i: (i, 0)),
                  pl.BlockSpec((tm, Fd), lambda i: (i, 0)),
                  pl.BlockSpec((1, Fd), lambda i: (0, 0)),
                  pl.BlockSpec((Fd, D), lambda i: (0, 0))],
        out_specs=pl.BlockSpec((tm, D), lambda i: (i, 0)),
        compiler_params=_cparams(("parallel",)),
        name="gelu_proj",
    )(a, b, bias, w)


def _cmp_attn_kernel(q_ref, ck_ref, cv_ref, ovn_ref, o_ref, imp_ref, *, H, slopes, qpos0, n_cmp, scale, tr):
    i = pl.program_id(1)
    ncp = ck_ref.shape[0]
    shift = int(math.log2(H))
    q = q_ref[...].astype(BF16)
    ck = ck_ref[...].astype(BF16)
    s = lax.dot_general(q, ck, (((1,), (1,)), ((), ())), preferred_element_type=F32) * scale
    rows = i * tr + lax.broadcasted_iota(jnp.int32, (tr, 1), 0)
    qpos = qpos0 + lax.shift_right_logical(rows, shift)
    n_idx = lax.broadcasted_iota(jnp.int32, (1, ncp), 1)
    dist = qpos - (n_idx * NSA_CMP_STRIDE + (NSA_CMP_LEN - 1))
    mask = jnp.logical_and(dist >= 0, n_idx < n_cmp)
    head = jnp.bitwise_and(rows, H - 1)
    slope = jnp.zeros((tr, 1), F32)
    for h, sl in enumerate(slopes):
        slope = jnp.where(head == h, np.float32(sl), slope)
    s = s - slope * dist.astype(F32)
    s = jnp.where(mask, s, NEG_INF)
    m = jnp.max(s, axis=-1, keepdims=True)
    e = jnp.where(mask, jnp.exp(s - m), 0.0)
    p = e / jnp.maximum(jnp.sum(e, axis=-1, keepdims=True), 1e-30)
    o_ref[...] = jnp.dot(p.astype(BF16), cv_ref[...].astype(BF16), preferred_element_type=F32)
    tq = tr // H
    head_sum = (lax.shift_right_logical(lax.broadcasted_iota(jnp.int32, (tq, tr), 1), shift)
                == lax.broadcasted_iota(jnp.int32, (tq, tr), 0)).astype(F32)
    p_sum = jnp.dot(head_sum, p, preferred_element_type=F32, precision=HIGHEST)
    imp_ref[...] = jnp.dot(p_sum, ovn_ref[...], preferred_element_type=F32, precision=HIGHEST)


def cmp_attention(q, ck, cv, ovn, *, H, slopes, qpos0, n_cmp, scale, tr):
    B, R, D = q.shape
    ncp = ck.shape[1]
    nsp = ovn.shape[1]
    tr = _pick_tile(R, tr)
    return pl.pallas_call(
        functools.partial(_cmp_attn_kernel, H=H, slopes=slopes, qpos0=qpos0, n_cmp=n_cmp, scale=scale, tr=tr),
        out_shape=(jax.ShapeDtypeStruct((B, R, D), F32), jax.ShapeDtypeStruct((B, R // H, nsp), F32)),
        grid=(B, R // tr),
        in_specs=[pl.BlockSpec((None, tr, D), lambda b, i: (b, i, 0)),
                  pl.BlockSpec((None, ncp, D), lambda b, i: (b, 0, 0)),
                  pl.BlockSpec((None, ncp, D), lambda b, i: (b, 0, 0)),
                  pl.BlockSpec((ncp, nsp), lambda b, i: (0, 0))],
        out_specs=(pl.BlockSpec((None, tr, D), lambda b, i: (b, i, 0)),
                   pl.BlockSpec((None, tr // H, nsp), lambda b, i: (b, i, 0))),
        compiler_params=_cparams(("parallel", "arbitrary")),
        name="nsa_cmp_attn",
    )(q, ck, cv, ovn)


def _merge_kernel(oa_ref, ob_ref, oc_ref, wb_ref, ga_ref, gb_ref, gc_ref, o_ref):
    acc = None
    for n, (o_r, g_r) in enumerate(((oa_ref, ga_ref), (ob_ref, gb_ref), (oc_ref, gc_ref))):
        up = jnp.dot(o_r[...].astype(BF16), wb_ref[n], preferred_element_type=F32)
        term = jax.nn.sigmoid(g_r[...]) * up
        acc = term if acc is None else acc + term
    o_ref[...] = acc.astype(o_ref.dtype)


def merge_branches(o_a, o_b, o_c, w_branch, proj, d_model, *, tm=512, tn=512):
    M = o_a.shape[0]
    tm = _pick_tile(M, tm)
    tn = _pick_tile(d_model, tn)
    nj = d_model // tn
    goff = C_MGATE // tn

    def gate_spec(n):
        return pl.BlockSpec((tm, tn), lambda i, j: (i, goff + n * nj + j))

    o_spec = pl.BlockSpec((tm, BRANCH_WIDTH), lambda i, j: (i, 0))
    return pl.pallas_call(
        _merge_kernel,
        out_shape=jax.ShapeDtypeStruct((M, d_model), BF16),
        grid=(M // tm, nj),
        in_specs=[o_spec, o_spec, o_spec,
                  pl.BlockSpec((N_BRANCH, BRANCH_WIDTH, tn), lambda i, j: (0, 0, j)),
                  gate_spec(0), gate_spec(1), gate_spec(2)],
        out_specs=pl.BlockSpec((tm, tn), lambda i, j: (i, j)),
        compiler_params=_cparams(("parallel", "arbitrary")),
        name="merge_branches",
    )(o_a, o_b, o_c, w_branch, proj, proj, proj)


def _layer_norm_rows(v, g, b):
    mu = jnp.mean(v, axis=-1, keepdims=True)
    var = jnp.mean(jnp.square(v - mu), axis=-1, keepdims=True)
    return (v - mu) * lax.rsqrt(var + LN_EPS) * g + b


def _out_ln_kernel(m_ref, w_ref, x_ref, g_ref, b_ref, o_ref, *, alpha):
    mix = jnp.dot(m_ref[...].astype(BF16), w_ref[...], preferred_element_type=F32)
    o_ref[...] = _layer_norm_rows(alpha * x_ref[...] + mix, g_ref[...], b_ref[...])


def out_proj_ln(merged, w_out, x, g, b, *, alpha, tm=256):
    M, D = x.shape
    tm = _pick_tile(M, tm)
    return pl.pallas_call(
        functools.partial(_out_ln_kernel, alpha=alpha),
        out_shape=jax.ShapeDtypeStruct((M, D), F32),
        grid=(M // tm,),
        in_specs=[pl.BlockSpec((tm, D), lambda i: (i, 0)),
                  pl.BlockSpec((D, D), lambda i: (0, 0)),
                  pl.BlockSpec((tm, D), lambda i: (i, 0)),
                  pl.BlockSpec((1, D), lambda i: (0, 0)),
                  pl.BlockSpec((1, D), lambda i: (0, 0))],
        out_specs=pl.BlockSpec((tm, D), lambda i: (i, 0)),
        compiler_params=_cparams(("parallel",)),
        name="out_proj_ln1",
    )(merged, w_out, x, g, b)


def _moe_up_kernel(te_ref, nt_ref, xs_ref, wg_ref, wu_ref, rw_ref, o_ref):
    i = pl.program_id(0)

    @pl.when(i < nt_ref[0])
    def _():
        x = xs_ref[...]
        gate = jnp.dot(x, wg_ref[...], preferred_element_type=F32)
        up = jnp.dot(x, wu_ref[...], preferred_element_type=F32)
        o_ref[...] = (jax.nn.silu(gate) * up * rw_ref[...]).astype(o_ref.dtype)

    @pl.when(i >= nt_ref[0])
    def _():
        o_ref[...] = jnp.zeros_like(o_ref)


def _moe_down_kernel(te_ref, nt_ref, h_ref, wd_ref, o_ref):
    i = pl.program_id(0)

    @pl.when(i < nt_ref[0])
    def _():
        o_ref[...] = jnp.dot(h_ref[...], wd_ref[...], preferred_element_type=F32)

    @pl.when(i >= nt_ref[0])
    def _():
        o_ref[...] = jnp.zeros_like(o_ref)


def moe_experts(xs, row_w, tile_expert, n_tiles, w_gate, w_up, w_down, *, tm):
    P, D = xs.shape
    Fe = w_gate.shape[2]
    nt = P // tm
    hidden = pl.pallas_call(
        _moe_up_kernel,
        out_shape=jax.ShapeDtypeStruct((P, Fe), BF16),
        grid_spec=pltpu.PrefetchScalarGridSpec(
            num_scalar_prefetch=2, grid=(nt,),
            in_specs=[pl.BlockSpec((tm, D), lambda i, te, n: (i, 0)),
                      pl.BlockSpec((None, D, Fe), lambda i, te, n: (te[i], 0, 0)),
                      pl.BlockSpec((None, D, Fe), lambda i, te, n: (te[i], 0, 0)),
                      pl.BlockSpec((tm, 1), lambda i, te, n: (i, 0))],
            out_specs=pl.BlockSpec((tm, Fe), lambda i, te, n: (i, 0))),
        compiler_params=_cparams(("arbitrary",)),
        name="moe_up",
    )(tile_expert, n_tiles, xs, w_gate, w_up, row_w)
    return pl.pallas_call(
        _moe_down_kernel,
        out_shape=jax.ShapeDtypeStruct((P, D), F32),
        grid_spec=pltpu.PrefetchScalarGridSpec(
            num_scalar_prefetch=2, grid=(nt,),
            in_specs=[pl.BlockSpec((tm, Fe), lambda i, te, n: (i, 0)),
                      pl.BlockSpec((None, Fe, D), lambda i, te, n: (te[i], 0, 0))],
            out_specs=pl.BlockSpec((tm, D), lambda i, te, n: (i, 0))),
        compiler_params=_cparams(("arbitrary",)),
        name="moe_down",
    )(tile_expert, n_tiles, hidden, w_down)


def _ffn_ln_kernel(h_ref, f_ref, p_ref, wg_ref, wp_ref, g_ref, b_ref, o_ref, *, alpha):
    h = h_ref[...]
    gate = jax.nn.sigmoid(jnp.dot(h.astype(BF16), wg_ref[...], preferred_element_type=F32))
    ple = gate * jnp.dot(p_ref[...].astype(BF16), wp_ref[...], preferred_element_type=F32)
    o_ref[...] = _layer_norm_rows(alpha * h + f_ref[...] + ple, g_ref[...], b_ref[...])


def ffn_ple_ln(h, ffn, p, w_pg, w_pp, g, b, *, alpha, tm=256):
    M, D = h.shape
    Pd = p.shape[1]
    tm = _pick_tile(M, tm)
    return pl.pallas_call(
        functools.partial(_ffn_ln_kernel, alpha=alpha),
        out_shape=jax.ShapeDtypeStruct((M, D), F32),
        grid=(M // tm,),
        in_specs=[pl.BlockSpec((tm, D), lambda i: (i, 0)),
                  pl.BlockSpec((tm, D), lambda i: (i, 0)),
                  pl.BlockSpec((tm, Pd), lambda i: (i, 0)),
                  pl.BlockSpec((D, D), lambda i: (0, 0)),
                  pl.BlockSpec((Pd, D), lambda i: (0, 0)),
                  pl.BlockSpec((1, D), lambda i: (0, 0)),
                  pl.BlockSpec((1, D), lambda i: (0, 0))],
        out_specs=pl.BlockSpec((tm, D), lambda i: (i, 0)),
        compiler_params=_cparams(("parallel",)),
        name="ffn_ple_ln2",
    )(h, ffn, p, w_pg, w_pp, g, b)


def _alibi_slopes(n):
    return tuple(np.asarray(2.0 ** (-8.0 * np.arange(1, n + 1) / n), dtype=np.float32).tolist())


def _rms_norm(x, g):
    return x * lax.rsqrt(jnp.square(x).mean(-1, keepdims=True) + RMS_EPS) * g


def _rope(x, pos):
    half = x.shape[-1] // 2
    freqs = ROPE_BASE ** (-jnp.arange(half, dtype=F32) / half)
    ang = pos.astype(F32)[:, None] * freqs[None, :]
    shape = (pos.shape[0],) + (1,) * (x.ndim - 3) + (half,)
    cos = jnp.cos(ang).reshape(shape)
    sin = jnp.sin(ang).reshape(shape)
    x1, x2 = x[..., :half], x[..., half:]
    return jnp.concatenate([x1 * cos - x2 * sin, x1 * sin + x2 * cos], -1)


def _pad_axis(x, axis, size):
    if x.shape[axis] == size:
        return x
    pad = [(0, 0)] * x.ndim
    pad[axis] = (0, size - x.shape[axis])
    return jnp.pad(x, pad)


def _selection_mask(top, ok, n_pad):
    hot = (top[..., None] == jnp.arange(n_pad, dtype=top.dtype)) & ok[..., None]
    return hot.any(-2).astype(F32)


def moba_attention(mq, k_new, v_new, k_old, v_old, q_pos0):
    B, T, _ = mq.shape
    H, KVH, HD = MOBA_HEADS, MOBA_KV_HEADS, HEAD_DIM
    G = H // KVH
    if k_old is None:
        k_full, v_full = k_new, v_new
    else:
        k_full = jnp.concatenate([k_old, k_new], axis=1)
        v_full = jnp.concatenate([v_old, v_new], axis=1)
    L = k_full.shape[1]
    nb = -(-L // MOBA_BLOCK)
    Lp = nb * MOBA_BLOCK
    kf = _pad_axis(k_full, 1, Lp).transpose(0, 2, 1, 3).reshape(B * KVH, Lp, HD)
    vf = _pad_axis(v_full, 1, Lp).transpose(0, 2, 1, 3).reshape(B * KVH, Lp, HD)
    q = mq.reshape(B, T, KVH, G, HD).transpose(0, 2, 1, 3, 4).reshape(B * KVH, T * G, HD)
    k_mean = block_mean(kf, MOBA_BLOCK)
    nbp = _round_up(nb, LANE)
    gate = bmm_f32(q, _pad_axis(k_mean.transpose(0, 2, 1), 2, nbp), name="moba_gate")[..., :nb]
    q_pos = q_pos0 + jnp.arange(T, dtype=jnp.int32)
    own = jnp.repeat(q_pos // MOBA_BLOCK, G)
    past_blk = jnp.arange(nb)[None, :] < own[:, None]
    gate = jnp.where(past_blk[None], gate, NEG_INF)
    _, top = lax.top_k(gate, min(MOBA_TOPK, nb))
    own_b = jnp.broadcast_to(own[None, :, None], top.shape[:2] + (1,)).astype(top.dtype)
    blocks = jnp.concatenate([top, own_b], -1)
    blk_ok = jnp.concatenate([top < own_b, jnp.ones_like(own_b, dtype=bool)], -1)
    sel = _selection_mask(blocks, blk_ok, nbp)
    out = flash(q, kf, vf, G=G, slopes=_alibi_slopes(H), kv_heads=KVH, qpos0=q_pos0, lk=L, sel=sel,
                sel_blk=MOBA_BLOCK, scale=HD ** -0.5, tr=1024, tk=MOBA_BLOCK, name="moba_attn")
    return out.reshape(B, KVH, T, G, HD).transpose(0, 2, 1, 3, 4).reshape(B, T, H * HD)


def nsa_attention(nq, ngate, cmp_k, cmp_v, sel_k, sel_v, win_k, win_v, win_pos0, q_pos0, T, cw):
    B = nq.shape[0]
    H, HD = NSA_HEADS, HEAD_DIM
    slopes = _alibi_slopes(H)
    scale = HD ** -0.5
    L = cmp_k.shape[1]
    q = nq.reshape(B, T * H, HD)
    m = L // NSA_CMP_STRIDE
    n_cmp = m - NSA_CMP_LEN // NSA_CMP_STRIDE + 1
    comp = []
    for kk, rows in enumerate((cmp_k, cmp_v)):
        sub = rows[:, :m * NSA_CMP_STRIDE].reshape(B * m, NSA_CMP_STRIDE * HD)
        part = mm(sub, cw['w1cat'][kk], tm=1024, tn=2 * NSA_CMP_HIDDEN, name="nsa_cmp_w1")
        pe_part = mm(cw['pe'][kk], cw['w1cat'][kk], tn=2 * NSA_CMP_HIDDEN, name="nsa_cmp_pos")
        bias = pe_part[0, :NSA_CMP_HIDDEN] + pe_part[1, NSA_CMP_HIDDEN:]
        part = part.reshape(B, m, 2 * NSA_CMP_HIDDEN)
        a0 = part[:, :, :NSA_CMP_HIDDEN]
        a1 = jnp.concatenate([part[:, 1:, NSA_CMP_HIDDEN:], jnp.zeros((B, 1, NSA_CMP_HIDDEN), F32)], axis=1)
        c = gelu_proj(a0.reshape(B * m, -1), a1.reshape(B * m, -1), bias[None, :], cw['w2'][kk])
        comp.append(c.reshape(B, m, HD))
    n_sel = -(-L // NSA_SEL_BLOCK)
    nsp = _round_up(n_sel, LANE)
    cst = np.arange(m) * NSA_CMP_STRIDE
    sst = np.arange(n_sel) * NSA_SEL_BLOCK
    ov = np.clip(np.minimum(cst[:, None] + NSA_CMP_LEN, sst[None, :] + NSA_SEL_BLOCK)
                 - np.maximum(cst[:, None], sst[None, :]), 0, None)
    ovn = np.zeros((m, nsp), np.float32)
    ovn[:n_cmp, :n_sel] = ov[:n_cmp].astype(np.float32) / NSA_CMP_LEN
    o_cmp, imp = cmp_attention(q, comp[0], comp[1], jnp.asarray(ovn), H=H, slopes=slopes, qpos0=q_pos0,
                               n_cmp=n_cmp, scale=scale, tr=512)
    imp = imp[..., :n_sel]
    q_pos = q_pos0 + jnp.arange(T, dtype=jnp.int32)
    own = q_pos // NSA_SEL_BLOCK
    jj = jnp.arange(n_sel)[None, :]
    forced = (jj == 0) | (jj == own[:, None]) | (jj == own[:, None] - 1)
    valid = jj <= own[:, None]
    score = jnp.where(valid[None], imp + jnp.where(forced[None], FORCE_BONUS, 0.0), NEG_INF)
    _, top = lax.top_k(score, min(NSA_TOPN, n_sel))
    top_ok = top <= own[None, :, None].astype(top.dtype)
    sel = _selection_mask(top, top_ok, nsp)
    sel = jnp.repeat(sel, H, axis=1)
    tk = 512
    Lp = _round_up(L, tk)
    o_sel = flash(q, _pad_axis(sel_k, 1, Lp), _pad_axis(sel_v, 1, Lp), G=H, slopes=slopes, qpos0=q_pos0, lk=L,
                  sel=sel, sel_blk=NSA_SEL_BLOCK, scale=scale, tr=1024, tk=tk, name="nsa_sel_attn")
    Lw = win_k.shape[1]
    tkw = 512 if Lw >= 512 else _round_up(Lw, LANE)
    Lwp = _round_up(Lw, tkw)
    o_win = flash(q, _pad_axis(win_k, 1, Lwp), _pad_axis(win_v, 1, Lwp), G=H, slopes=slopes, qpos0=q_pos0,
                  kpos0=win_pos0, lk=Lw, window=NSA_WINDOW, scale=scale, tr=1024, tk=tkw, name="nsa_win_attn")
    g = jax.nn.sigmoid(ngate.reshape(B, T, 3, H))[..., None]
    o = (g[:, :, 0] * o_cmp.reshape(B, T, H, HD) + g[:, :, 1] * o_sel.reshape(B, T, H, HD)
         + g[:, :, 2] * o_win.reshape(B, T, H, HD))
    return o.reshape(B, T, H * HD)


def mla_attention(q_abs, lat_full, q_pos0, T):
    L = lat_full.shape[1]
    tk = 512
    Lp = _round_up(L, tk)
    kf = _pad_axis(lat_full, 1, Lp)
    return flash(q_abs, kf, kf[..., :MLA_KV_LORA], G=MLA_HEADS, slopes=None, qpos0=q_pos0, lk=L,
                 scale=(MLA_NOPE + MLA_ROPE) ** -0.5, tr=1024, tk=tk, name="mla_attn")


def token_mixer(x2, B, T, q_pos0, past, lw):
    N = B * T
    pos = q_pos0 + jnp.arange(T, dtype=jnp.int32)
    proj = mm(x2, lw['w_in'], tm=1024, tn=512, name="in_proj")
    mq = proj[:, C_MQ:C_MQ + 512].reshape(B, T, 512)
    moba_rows = proj[:, C_MKV:C_MKV + 256].reshape(B, T, MOBA_KV_HEADS, 2, HEAD_DIM)
    nq = proj[:, C_NQ:C_NQ + 512].reshape(B, T, 512)
    nsa_kv = proj[:, C_NKV:C_NKV + 384].reshape(B, T, 6, HEAD_DIM)
    ngate = proj[:, C_NGATE:C_NGATE + 3 * NSA_HEADS]
    qa = proj[:, C_QA:C_QA + MLA_Q_LORA]
    kva = proj[:, C_KVA:C_KVA + MLA_KV_LORA + MLA_ROPE].reshape(B, T, -1)
    nsa_rows, win_rows = nsa_kv[:, :, :4], nsa_kv[:, :, 4:]
    mla_rows = jnp.concatenate([_rms_norm(kva[..., :MLA_KV_LORA], lw['mla_kv_norm']),
                                _rope(kva[..., MLA_KV_LORA:], pos)], -1)
    q_mla = mm(_rms_norm(qa, lw['mla_q_norm']), lw['mla_w_qb'], name="mla_q_up")
    q_mla = q_mla.reshape(B, T, MLA_HEADS, MLA_NOPE + MLA_ROPE)
    q_lat = mm(q_mla[..., :MLA_NOPE].reshape(N, -1), lw['mla_w_uk_bd'], name="mla_q_absorb")
    q_rope = _rope(q_mla[..., MLA_NOPE:], pos)
    q_abs = jnp.concatenate([q_lat.reshape(B, T, MLA_HEADS, MLA_KV_LORA), q_rope], -1)
    q_abs = q_abs.reshape(B, T * MLA_HEADS, MLA_KV_LORA + MLA_ROPE)

    if past is None:
        win_keys, win_pos0 = win_rows, 0
        new_win = win_rows[:, -min(NSA_WINDOW, T):]
        o_a = moba_attention(mq, moba_rows[:, :, :, 0], moba_rows[:, :, :, 1], None, None, q_pos0)
        o_b = nsa_attention(nq, ngate, nsa_rows[:, :, 0], nsa_rows[:, :, 1], nsa_rows[:, :, 2], nsa_rows[:, :, 3],
                            win_keys[:, :, 0], win_keys[:, :, 1], win_pos0, q_pos0, T, lw['cmp'])
        lat_full = mla_rows
    else:
        old_m, old_n, old_l, buf = past
        wb = buf.shape[1]
        win_keys = jnp.concatenate([buf, win_rows], axis=1)
        win_pos0 = q_pos0 - wb
        new_win = win_keys[:, -wb:]
        o_a = moba_attention(mq, moba_rows[:, :, :, 0], moba_rows[:, :, :, 1], old_m[:, :, :, 0], old_m[:, :, :, 1],
                             q_pos0)
        full_n = jnp.concatenate([old_n, nsa_rows], axis=1)
        o_b = nsa_attention(nq, ngate, full_n[:, :, 0], full_n[:, :, 1], full_n[:, :, 2], full_n[:, :, 3],
                            win_keys[:, :, 0], win_keys[:, :, 1], win_pos0, q_pos0, T, lw['cmp'])
        lat_full = jnp.concatenate([old_l, mla_rows], axis=1)
    o_lat = mla_attention(q_abs, lat_full, q_pos0, T)
    o_c = mm(o_lat.reshape(N, MLA_HEADS * MLA_KV_LORA), lw['mla_w_uv_bd'], name="mla_v_up")
    d_model = x2.shape[1]
    merged = merge_branches(o_a.reshape(N, -1), o_b.reshape(N, -1), o_c, lw['w_branch'], proj, d_model)
    return merged, moba_rows, nsa_rows, mla_rows, new_win


def moe_ffn(h, router_w_pad, router_bias, lw, *, tm):
    n, D = h.shape
    per_group = N_EXPERTS // N_GROUPS
    logits = mm(h, router_w_pad, precision=HIGHEST, name="router")[:, :N_EXPERTS]
    aff = jax.nn.sigmoid(logits)
    biased = aff + router_bias
    group_score = lax.top_k(biased.reshape(n, N_GROUPS, per_group), TOPK_EXPERTS)[0].sum(-1)
    _, group_sel = lax.top_k(group_score, 1)
    in_group = (jnp.arange(N_EXPERTS) // per_group)[None, :] == group_sel
    _, expert_idx = lax.top_k(jnp.where(in_group, biased, NEG_INF), TOPK_EXPERTS)
    w = jnp.take_along_axis(aff, expert_idx, axis=-1)
    w = w / w.sum(-1, keepdims=True)
    A = n * TOPK_EXPERTS
    e_flat = expert_idx.reshape(A).astype(jnp.int32)
    w_flat = w.reshape(A)
    order = jnp.argsort(e_flat, stable=True)
    e_sorted = e_flat[order]
    counts = jnp.sum(e_flat[:, None] == jnp.arange(N_EXPERTS, dtype=jnp.int32)[None, :], axis=0).astype(jnp.int32)
    padded = -(-counts // tm) * tm
    pend = jnp.cumsum(padded)
    pstart = pend - padded
    start = jnp.cumsum(counts) - counts
    dest_sorted = pstart[e_sorted] + (jnp.arange(A, dtype=jnp.int32) - start[e_sorted])
    P = _round_up(A, tm) + N_EXPERTS * tm
    slot_token = jnp.zeros((P,), jnp.int32).at[dest_sorted].set((order // TOPK_EXPERTS).astype(jnp.int32))
    slot_w = jnp.zeros((P,), F32).at[dest_sorted].set(w_flat[order])
    pos = jnp.zeros((A,), jnp.int32).at[order].set(dest_sorted).reshape(n, TOPK_EXPERTS)
    tile_start = jnp.arange(P // tm, dtype=jnp.int32) * tm
    tile_expert = jnp.minimum(jnp.searchsorted(pend, tile_start, side='right'), N_EXPERTS - 1).astype(jnp.int32)
    n_tiles = (pend[-1] // tm).astype(jnp.int32).reshape(1)
    xs = h.astype(BF16)[slot_token]
    y = moe_experts(xs, slot_w[:, None], tile_expert, n_tiles, lw['moe_w_gate'], lw['moe_w_up'], lw['moe_w_down'],
                    tm=tm)
    return y[pos[:, 0]] + y[pos[:, 1]]


def decoder_layer(x2, p2, B, T, q_pos0, past, lw, router_w_pad, router_bias, alpha, moe_tm):
    merged, moba_r, nsa_r, mla_r, win_s = token_mixer(x2, B, T, q_pos0, past, lw)
    h = out_proj_ln(merged, lw['w_out'], x2, lw['ln1_g'], lw['ln1_b'], alpha=alpha)
    ffn = moe_ffn(h, router_w_pad, router_bias, lw, tm=moe_tm)
    y = ffn_ple_ln(h, ffn, p2, lw['ple_w_gate'], lw['ple_w_proj'], lw['ln2_g'], lw['ln2_b'], alpha=alpha)
    return y, (moba_r, nsa_r, mla_r, win_s)


def _pack_w_in(w_in, d_model):
    sizes = (512, 256, 512, 384, 3 * NSA_HEADS, MLA_Q_LORA, MLA_KV_LORA + MLA_ROPE, N_BRANCH * d_model)
    starts = (C_MQ, C_MKV, C_NQ, C_NKV, C_NGATE, C_QA, C_KVA, C_MGATE)
    total = C_MGATE + N_BRANCH * d_model
    total = _round_up(total, 512)
    out = jnp.zeros(w_in.shape[:2] + (total,), BF16)
    off = 0
    for s, c in zip(sizes, starts):
        out = lax.dynamic_update_slice_in_dim(out, w_in[..., off:off + s].astype(BF16), c, axis=2)
        off += s
    return out


def kernel(x_prompt, x_sample, cache_moba, cache_nsa, cache_mla, state_nsa_win, page_table, p_prompt, p_sample,
           w_in, mla_q_norm, mla_w_qb, mla_kv_norm, mla_w_uk, mla_w_uv, nsa_cmp_pos, nsa_cmp_w1, nsa_cmp_w2,
           w_branch, w_out, ln1_g, ln1_b, router_w, router_bias, moe_w_gate, moe_w_up, moe_w_down,
           ple_w_proj, ple_w_gate, ln2_g, ln2_b):
    depth = w_in.shape[0]
    Bp, Tp, D = x_prompt.shape
    Bs, Ts, _ = x_sample.shape
    n_pages, page = page_table.shape[1], cache_moba.shape[2]
    past_len = n_pages * page
    alpha = (2 * depth) ** 0.25
    halves = NSA_CMP_LEN // NSA_CMP_STRIDE

    w_in_p = _pack_w_in(w_in, D)
    eye_h = jnp.eye(MLA_HEADS, dtype=F32)
    w_uk_bd = jnp.einsum('lchn,hg->lhngc', mla_w_uk, eye_h).reshape(depth, MLA_HEADS * MLA_NOPE,
                                                                   MLA_HEADS * MLA_KV_LORA).astype(BF16)
    w_uv_bd = jnp.einsum('lchv,hg->lhcgv', mla_w_uv, eye_h).reshape(depth, MLA_HEADS * MLA_KV_LORA,
                                                                   MLA_HEADS * MLA_V).astype(BF16)
    w1cat = nsa_cmp_w1.reshape(depth, 2, halves, NSA_CMP_STRIDE * HEAD_DIM, NSA_CMP_HIDDEN)
    w1cat = w1cat.transpose(0, 1, 3, 2, 4).reshape(depth, 2, NSA_CMP_STRIDE * HEAD_DIM, halves * NSA_CMP_HIDDEN)
    pe = nsa_cmp_pos.reshape(depth, 2, halves, NSA_CMP_STRIDE * HEAD_DIM)
    pe = jnp.concatenate([pe, jnp.zeros((depth, 2, 8 - halves, NSA_CMP_STRIDE * HEAD_DIM), F32)], axis=2)
    router_w_pad = _pad_axis(router_w, 1, LANE)
    big = dict(w_branch=w_branch.astype(BF16), w_out=w_out.astype(BF16), moe_w_gate=moe_w_gate.astype(BF16),
               moe_w_up=moe_w_up.astype(BF16), moe_w_down=moe_w_down.astype(BF16),
               ple_w_proj=ple_w_proj.astype(BF16), ple_w_gate=ple_w_gate.astype(BF16),
               mla_w_qb=mla_w_qb.astype(BF16))

    xp = x_prompt.reshape(Bp * Tp, D)
    xs = x_sample.reshape(Bs * Ts, D)
    st_p, st_s = [], []
    for i in range(depth):
        lw = {k: v[i] for k, v in big.items()}
        lw.update(w_in=w_in_p[i], mla_w_uk_bd=w_uk_bd[i], mla_w_uv_bd=w_uv_bd[i],
                  mla_q_norm=mla_q_norm[i], mla_kv_norm=mla_kv_norm[i],
                  cmp=dict(w1cat=w1cat[i], pe=pe[i], w2=nsa_cmp_w2[i]),
                  ln1_g=ln1_g[i][None], ln1_b=ln1_b[i][None], ln2_g=ln2_g[i][None], ln2_b=ln2_b[i][None])
        old_m = cache_moba[i][page_table].reshape(Bs, past_len, MOBA_KV_HEADS, 2, HEAD_DIM)
        old_n = cache_nsa[i][page_table].reshape(Bs, past_len, 4, HEAD_DIM)
        old_l = cache_mla[i][page_table].reshape(Bs, past_len, MLA_KV_LORA + MLA_ROPE)
        past = (old_m, old_n, old_l, state_nsa_win[i])
        xp, sp = decoder_layer(xp, p_prompt[i].reshape(Bp * Tp, -1), Bp, Tp, 0, None, lw, router_w_pad, router_bias,
                               alpha, 256)
        xs, ss = decoder_layer(xs, p_sample[i].reshape(Bs * Ts, -1), Bs, Ts, past_len, past, lw, router_w_pad,
                               router_bias, alpha, 128)
        st_p.append(sp)
        st_s.append(ss)
    outs = [xp.reshape(Bp, Tp, D), xs.reshape(Bs, Ts, D)]
    for c in range(4):
        outs.append(jnp.stack([s[c] for s in st_p]))
        outs.append(jnp.stack([s[c] for s in st_s]))
    return tuple(outs)
```

```python
import functools
import math

import numpy as np
import jax
import jax.numpy as jnp
from jax import lax
from jax.experimental import pallas as pl
from jax.experimental.pallas import tpu as pltpu

F32 = jnp.float32
BF16 = jnp.bfloat16
HIGHEST = lax.Precision.HIGHEST

HEAD_DIM = 64
MOBA_HEADS = 8
MOBA_KV_HEADS = 2
MOBA_BLOCK = 256
MOBA_TOPK = 3
NSA_HEADS = 8
NSA_CMP_LEN = 32
NSA_CMP_STRIDE = 16
NSA_CMP_HIDDEN = 128
NSA_SEL_BLOCK = 64
NSA_TOPN = 16
NSA_WINDOW = 512
MLA_HEADS = 8
MLA_Q_LORA = 384
MLA_KV_LORA = 128
MLA_NOPE = 64
MLA_ROPE = 32
MLA_V = 64
ROPE_BASE = 10000.0
N_BRANCH = 3
BRANCH_WIDTH = 512
N_EXPERTS = 16
N_GROUPS = 4
TOPK_EXPERTS = 2
LN_EPS = 1e-5
RMS_EPS = 1e-6
NEG_INF = -1e30
FORCE_BONUS = 1e4

LANE = 128
VMEM_LIMIT = 52 * 1024 * 1024

C_MQ = 0
C_MKV = 512
C_NQ = 768
C_NKV = 1280
C_NGATE = 1664
C_QA = 1792
C_KVA = 2176
C_MGATE = 2560


def _cparams(sem):
    return pltpu.CompilerParams(dimension_semantics=sem, vmem_limit_bytes=VMEM_LIMIT)


def _round_up(x, m):
    return -(-x // m) * m


def _pick_tile(n, pref):
    t = min(pref, n)
    while n % t:
        t //= 2
    return t


def _mm_kernel(x_ref, w_ref, o_ref, *, precision):
    x = x_ref[...]
    w = w_ref[...]
    if precision is None:
        x = x.astype(BF16)
        w = w.astype(BF16)
    o_ref[...] = jnp.dot(x, w, preferred_element_type=F32, precision=precision).astype(o_ref.dtype)


def mm(x, w, *, tm=512, tn=512, precision=None, out_dtype=F32, name="mm"):
    M, K = x.shape
    N = w.shape[1]
    tm = _pick_tile(M, tm)
    tn = _pick_tile(N, tn)
    return pl.pallas_call(
        functools.partial(_mm_kernel, precision=precision),
        out_shape=jax.ShapeDtypeStruct((M, N), out_dtype),
        grid=(M // tm, N // tn),
        in_specs=[pl.BlockSpec((tm, K), lambda i, j: (i, 0)),
                  pl.BlockSpec((K, tn), lambda i, j: (0, j))],
        out_specs=pl.BlockSpec((tm, tn), lambda i, j: (i, j)),
        compiler_params=_cparams(("parallel", "arbitrary")),
        name=name,
    )(x, w)


def _bmm_kernel(x_ref, w_ref, o_ref):
    o_ref[...] = jnp.dot(x_ref[...], w_ref[...], preferred_element_type=F32, precision=HIGHEST)


def bmm_f32(x, w, name="bmm"):
    B, M, K = x.shape
    N = w.shape[2]
    tm = _pick_tile(M, 1024)
    return pl.pallas_call(
        _bmm_kernel,
        out_shape=jax.ShapeDtypeStruct((B, M, N), F32),
        grid=(B, M // tm),
        in_specs=[pl.BlockSpec((None, tm, K), lambda b, i: (b, i, 0)),
                  pl.BlockSpec((None, K, N), lambda b, i: (b, 0, 0))],
        out_specs=pl.BlockSpec((None, tm, N), lambda b, i: (b, i, 0)),
        compiler_params=_cparams(("parallel", "arbitrary")),
        name=name,
    )(x, w)


def _block_mean_kernel(k_ref, o_ref, *, inv):
    o_ref[...] = jnp.sum(k_ref[...], axis=0, keepdims=True) * inv


def block_mean(k, blk):
    B, L, D = k.shape
    nb = L // blk
    out = pl.pallas_call(
        functools.partial(_block_mean_kernel, inv=1.0 / blk),
        out_shape=jax.ShapeDtypeStruct((B, nb, 1, D), F32),
        grid=(B, nb),
        in_specs=[pl.BlockSpec((None, blk, D), lambda b, n: (b, n, 0))],
        out_specs=pl.BlockSpec((None, None, 1, D), lambda b, n: (b, n, 0, 0)),
        compiler_params=_cparams(("parallel", "arbitrary")),
        name="block_mean",
    )(k)
    return out.reshape(B, nb, D)


def _flash_kernel(*refs, G, slopes, kv_heads, qpos0, kpos0, lk, window, sel_blk, scale, tr, tk, nkt):
    if sel_blk is None:
        q_ref, k_ref, v_ref, o_ref, m_sc, l_sc, acc_sc = refs
        sel_ref = None
    else:
        q_ref, k_ref, v_ref, sel_ref, o_ref, m_sc, l_sc, acc_sc = refs
    b = pl.program_id(0)
    i = pl.program_id(1)
    j = pl.program_id(2)
    shift = int(math.log2(G))

    @pl.when(j == 0)
    def _():
        m_sc[...] = jnp.full_like(m_sc, NEG_INF)
        l_sc[...] = jnp.zeros_like(l_sc)
        acc_sc[...] = jnp.zeros_like(acc_sc)

    q_lo = qpos0 + lax.shift_right_logical(i * tr, shift)
    q_hi = qpos0 + lax.shift_right_logical(i * tr + (tr - 1), shift)
    k_lo = kpos0 + j * tk
    needed = jnp.logical_and(k_lo <= q_hi, j * tk < lk)
    if window is not None:
        needed = jnp.logical_and(needed, k_lo + (tk - 1) > q_lo - window)

    @pl.when(needed)
    def _():
        q = q_ref[...].astype(BF16)
        k = k_ref[...].astype(BF16)
        s = lax.dot_general(q, k, (((1,), (1,)), ((), ())), preferred_element_type=F32) * scale
        rows = i * tr + lax.broadcasted_iota(jnp.int32, (tr, 1), 0)
        qpos = qpos0 + lax.shift_right_logical(rows, shift)
        kidx = j * tk + lax.broadcasted_iota(jnp.int32, (1, tk), 1)
        dist = qpos - (kpos0 + kidx)
        mask = jnp.logical_and(dist >= 0, kidx < lk)
        if window is not None:
            mask = jnp.logical_and(mask, dist < window)
        if slopes is not None:
            g = jnp.bitwise_and(rows, G - 1)
            head = g if kv_heads == 1 else g + G * lax.rem(b, kv_heads)
            slope = jnp.zeros((tr, 1), F32)
            for h, sl in enumerate(slopes):
                slope = jnp.where(head == h, np.float32(sl), slope)
            s = s - slope * dist.astype(F32)
        if sel_ref is not None:
            nbp = sel_ref.shape[-1]
            blk_of_key = lax.shift_right_logical(kidx, int(math.log2(sel_blk)))
            expand = (lax.broadcasted_iota(jnp.int32, (nbp, tk), 0) == blk_of_key).astype(BF16)
            picked = jnp.dot(sel_ref[...].astype(BF16), expand, preferred_element_type=F32)
            mask = jnp.logical_and(mask, picked > 0.5)
        s = jnp.where(mask, s, NEG_INF)
        m_prev = m_sc[...]
        m_new = jnp.maximum(m_prev, jnp.max(s, axis=-1, keepdims=True))
        alpha = jnp.exp(m_prev - m_new)
        p = jnp.where(mask, jnp.exp(s - m_new), 0.0)
        l_sc[...] = alpha * l_sc[...] + jnp.sum(p, axis=-1, keepdims=True)
        acc_sc[...] = alpha * acc_sc[...] + jnp.dot(p.astype(BF16), v_ref[...].astype(BF16),
                                                    preferred_element_type=F32)
        m_sc[...] = m_new

    @pl.when(j == nkt - 1)
    def _():
        o_ref[...] = acc_sc[...] / jnp.maximum(l_sc[...], 1e-30)


def flash(q, k, v, *, G, slopes, kv_heads=1, qpos0, kpos0=0, lk, window=None, sel=None, sel_blk=None,
          scale, tr, tk, name):
    B, R, Dk = q.shape
    Lk, Dv = v.shape[1], v.shape[2]
    tr = _pick_tile(R, tr)
    assert Lk % tk == 0 and G & (G - 1) == 0
    nkt = Lk // tk
    kern = functools.partial(_flash_kernel, G=G, slopes=slopes, kv_heads=kv_heads, qpos0=qpos0, kpos0=kpos0,
                             lk=lk, window=window, sel_blk=sel_blk, scale=scale, tr=tr, tk=tk, nkt=nkt)
    in_specs = [pl.BlockSpec((None, tr, Dk), lambda b, i, j: (b, i, 0)),
                pl.BlockSpec((None, tk, Dk), lambda b, i, j: (b, j, 0)),
                pl.BlockSpec((None, tk, Dv), lambda b, i, j: (b, j, 0))]
    args = [q, k, v]
    if sel is not None:
        in_specs.append(pl.BlockSpec((None, tr, sel.shape[-1]), lambda b, i, j: (b, i, 0)))
        args.append(sel)
    return pl.pallas_call(
        kern,
        out_shape=jax.ShapeDtypeStruct((B, R, Dv), F32),
        grid=(B, R // tr, nkt),
        in_specs=in_specs,
        out_specs=pl.BlockSpec((None, tr, Dv), lambda b, i, j: (b, i, 0)),
        scratch_shapes=[pltpu.VMEM((tr, 1), F32), pltpu.VMEM((tr, 1), F32), pltpu.VMEM((tr, Dv), F32)],
        compiler_params=_cparams(("parallel", "parallel", "arbitrary")),
        name=name,
    )(*args)


def _gelu_proj_kernel(a_ref, b_ref, bias_ref, w_ref, o_ref):
    pre = a_ref[...] + b_ref[...] + bias_ref[...]
    act = jax.nn.gelu(pre)
    o_ref[...] = jnp.dot(act.astype(BF16), w_ref[...].astype(BF16), preferred_element_type=F32)


def gelu_proj(a, b, bias, w):
    M, Fd = a.shape
    D = w.shape[1]
    tm = _pick_tile(M, 1024)
    return pl.pallas_call(
        _gelu_proj_kernel,
        out_shape=jax.ShapeDtypeStruct((M, D), F32),
        grid=(M // tm,),
        in_specs=[pl.BlockSpec((tm, Fd), lambda i: (i, 0)),
                  pl.BlockSpec((tm, Fd), lambda i: (i, 0)),
                  pl.BlockSpec((1, Fd), lambda i: (0, 0)),
                  pl.BlockSpec((Fd, D), lambda i: (0, 0))],
        out_specs=pl.BlockSpec((tm, D), lambda i: (i, 0)),
        compiler_params=_cparams(("parallel",)),
        name="gelu_proj",
    )(a, b, bias, w)


def _cmp_attn_kernel(q_ref, ck_ref, cv_ref, ovn_ref, o_ref, imp_ref, *, H, slopes, qpos0, n_cmp, scale, tr):
    i = pl.program_id(1)
    ncp = ck_ref.shape[0]
    shift = int(math.log2(H))
    q = q_ref[...].astype(BF16)
    ck = ck_ref[...].astype(BF16)
    s = lax.dot_general(q, ck, (((1,), (1,)), ((), ())), preferred_element_type=F32) * scale
    rows = i * tr + lax.broadcasted_iota(jnp.int32, (tr, 1), 0)
    qpos = qpos0 + lax.shift_right_logical(rows, shift)
    n_idx = lax.broadcasted_iota(jnp.int32, (1, ncp), 1)
    dist = qpos - (n_idx * NSA_CMP_STRIDE + (NSA_CMP_LEN - 1))
    mask = jnp.logical_and(dist >= 0, n_idx < n_cmp)
    head = jnp.bitwise_and(rows, H - 1)
    slope = jnp.zeros((tr, 1), F32)
    for h, sl in enumerate(slopes):
        slope = jnp.where(head == h, np.float32(sl), slope)
    s = s - slope * dist.astype(F32)
    s = jnp.where(mask, s, NEG_INF)
    m = jnp.max(s, axis=-1, keepdims=True)
    e = jnp.where(mask, jnp.exp(s - m), 0.0)
    p = e / jnp.maximum(jnp.sum(e, axis=-1, keepdims=True), 1e-30)
    o_ref[...] = jnp.dot(p.astype(BF16), cv_ref[...].astype(BF16), preferred_element_type=F32)
    tq = tr // H
    head_sum = (lax.shift_right_logical(lax.broadcasted_iota(jnp.int32, (tq, tr), 1), shift)
                == lax.broadcasted_iota(jnp.int32, (tq, tr), 0)).astype(F32)
    p_sum = jnp.dot(head_sum, p, preferred_element_type=F32, precision=HIGHEST)
    imp_ref[...] = jnp.dot(p_sum, ovn_ref[...], preferred_element_type=F32, precision=HIGHEST)


def cmp_attention(q, ck, cv, ovn, *, H, slopes, qpos0, n_cmp, scale, tr):
    B, R, D = q.shape
    ncp = ck.shape[1]
    nsp = ovn.shape[1]
    tr = _pick_tile(R, tr)
    return pl.pallas_call(
        functools.partial(_cmp_attn_kernel, H=H, slopes=slopes, qpos0=qpos0, n_cmp=n_cmp, scale=scale, tr=tr),
        out_shape=(jax.ShapeDtypeStruct((B, R, D), F32), jax.ShapeDtypeStruct((B, R // H, nsp), F32)),
        grid=(B, R // tr),
        in_specs=[pl.BlockSpec((None, tr, D), lambda b, i: (b, i, 0)),
                  pl.BlockSpec((None, ncp, D), lambda b, i: (b, 0, 0)),
                  pl.BlockSpec((None, ncp, D), lambda b, i: (b, 0, 0)),
                  pl.BlockSpec((ncp, nsp), lambda b, i: (0, 0))],
        out_specs=(pl.BlockSpec((None, tr, D), lambda b, i: (b, i, 0)),
                   pl.BlockSpec((None, tr // H, nsp), lambda b, i: (b, i, 0))),
        compiler_params=_cparams(("parallel", "arbitrary")),
        name="nsa_cmp_attn",
    )(q, ck, cv, ovn)


PAGES_PER_STEP = 8


def _page_specs(rows, row_block, layer, n_pages):
    def spec(p):
        return pl.BlockSpec((None, None, rows, LANE),
                            lambda b, j, tbl: (layer, tbl[b * n_pages + j * PAGES_PER_STEP + p], row_block, 0))
    return [spec(p) for p in range(PAGES_PER_STEP)]


def _row_slopes(head, slopes, shape):
    slope = jnp.zeros(shape, F32)
    for h, sl in enumerate(slopes):
        slope = jnp.where(head == h, np.float32(sl), slope)
    return slope


def _nt_dot(a, b):
    return lax.dot_general(a, b, (((1,), (1,)), ((), ())), preferred_element_type=F32)


def _moba_paged_kernel(tbl_ref, q_ref, new_ref, *rest, slopes, past_len, n_steps, T, G, scale):
    pages = rest[:PAGES_PER_STEP]
    o_ref, m_part, l_part, kmean, o_part = rest[PAGES_PER_STEP:]
    j = pl.program_id(1)
    KVH = MOBA_KV_HEADS
    R = T * G
    HD = HEAD_DIM
    blocks_per_step = PAGES_PER_STEP * LANE // MOBA_BLOCK
    pages_per_block = MOBA_BLOCK // LANE
    nb_past = past_len // MOBA_BLOCK
    lane = lax.broadcasted_iota(jnp.int32, (1, LANE), 1)
    rows = lax.broadcasted_iota(jnp.int32, (R, 1), 0)
    t_of_row = lax.shift_right_logical(rows, int(math.log2(G)))
    qpos = past_len + t_of_row

    @pl.when(j == 0)
    def _():
        m_part[...] = jnp.full_like(m_part, NEG_INF)
        l_part[...] = jnp.zeros_like(l_part)
        kmean[...] = jnp.zeros_like(kmean)

    for kvh in range(KVH):
        rs = pl.ds(kvh * R, R)
        slope = _row_slopes(jnp.bitwise_and(rows, G - 1) + kvh * G, slopes, (R, 1))
        qf = q_ref[rs, :]
        qb = qf.astype(BF16)
        for c in range(blocks_per_step):
            pg = [pages[c * pages_per_block + u] for u in range(pages_per_block)]
            kt = jnp.concatenate([p_[pl.ds(kvh * 2 * HD, HD), :] for p_ in pg], axis=1)
            vt = jnp.concatenate([p_[pl.ds(kvh * 2 * HD + HD, HD), :] for p_ in pg], axis=1)
            blk = j * blocks_per_step + c
            kpos = blk * MOBA_BLOCK + lax.broadcasted_iota(jnp.int32, (1, MOBA_BLOCK), 1)
            s = jnp.dot(qb, kt.astype(BF16), preferred_element_type=F32) * scale
            s = s - slope * (qpos - kpos).astype(F32)
            m = jnp.max(s, axis=-1, keepdims=True)
            p = jnp.exp(s - m)
            here = lane == blk
            m_part[rs, :] = jnp.where(here, m, m_part[rs, :])
            l_part[rs, :] = jnp.where(here, jnp.sum(p, axis=-1, keepdims=True), l_part[rs, :])
            o_part[blk, rs, :] = _nt_dot(p.astype(BF16), vt.astype(BF16))
            kmean[kvh] = jnp.where(here, jnp.sum(kt, axis=-1, keepdims=True) * (1.0 / MOBA_BLOCK), kmean[kvh])

        @pl.when(j == n_steps - 1)
        def _():
            gate = jnp.dot(qf, kmean[kvh], preferred_element_type=F32, precision=HIGHEST)
            past = lane < nb_past
            gate = jnp.where(past, gate, NEG_INF)
            rank = jnp.zeros((R, LANE), jnp.int32)
            for c2 in range(nb_past):
                col = gate[:, c2:c2 + 1]
                beats = jnp.logical_or(col > gate, jnp.logical_and(col == gate, lane > c2))
                rank = rank + beats.astype(jnp.int32)
            sel = jnp.logical_and(past, rank < MOBA_TOPK)
            k_new = new_ref[:, pl.ds(kvh * 2 * HD, HD)]
            v_new = new_ref[:, pl.ds(kvh * 2 * HD + HD, HD)]
            dist = t_of_row - lax.broadcasted_iota(jnp.int32, (1, T), 1)
            ok = dist >= 0
            s_own = _nt_dot(qb, k_new.astype(BF16)) * scale - slope * dist.astype(F32)
            s_own = jnp.where(ok, s_own, NEG_INF)
            mp = m_part[rs, :]
            m_all = jnp.maximum(jnp.max(s_own, axis=-1, keepdims=True),
                                jnp.max(jnp.where(sel, mp, NEG_INF), axis=-1, keepdims=True))
            p_own = jnp.where(ok, jnp.exp(s_own - m_all), 0.0)
            w = jnp.where(sel, jnp.exp(mp - m_all), 0.0)
            l_all = jnp.sum(p_own, axis=-1, keepdims=True) + jnp.sum(w * l_part[rs, :], axis=-1, keepdims=True)
            acc = jnp.dot(p_own.astype(BF16), v_new.astype(BF16), preferred_element_type=F32)
            for c2 in range(nb_past):
                acc = acc + w[:, c2:c2 + 1] * o_part[c2, rs, :]
            o_ref[rs, :] = acc / jnp.maximum(l_all, 1e-30)


def moba_paged(q, proj, cache_t, table, layer, *, past_len, T):
    B, R2, HD = q.shape
    n_pages = past_len // LANE
    n_steps = n_pages // PAGES_PER_STEP
    G = MOBA_HEADS // MOBA_KV_HEADS
    nb_past = past_len // MOBA_BLOCK
    assert n_pages % PAGES_PER_STEP == 0 and past_len % MOBA_BLOCK == 0 and T <= MOBA_BLOCK and nb_past <= LANE
    width = MOBA_KV_HEADS * 2 * HD
    kern = functools.partial(_moba_paged_kernel, slopes=_alibi_slopes(MOBA_HEADS), past_len=past_len,
                             n_steps=n_steps, T=T, G=G, scale=HD ** -0.5)
    return pl.pallas_call(
        kern,
        out_shape=jax.ShapeDtypeStruct((B, R2, HD), F32),
        grid_spec=pltpu.PrefetchScalarGridSpec(
            num_scalar_prefetch=1, grid=(B, n_steps),
            in_specs=[pl.BlockSpec((None, R2, HD), lambda b, j, tbl: (b, 0, 0)),
                      pl.BlockSpec((T, width), lambda b, j, tbl: (b, C_MKV // width))]
            + _page_specs(width, 0, layer, n_pages),
            out_specs=pl.BlockSpec((None, R2, HD), lambda b, j, tbl: (b, 0, 0)),
            scratch_shapes=[pltpu.VMEM((R2, LANE), F32), pltpu.VMEM((R2, LANE), F32),
                            pltpu.VMEM((MOBA_KV_HEADS, HD, LANE), F32), pltpu.VMEM((nb_past, R2, HD), F32)]),
        compiler_params=_cparams(("parallel", "arbitrary")),
        name="moba_paged",
    )(table, q, proj, *([cache_t] * PAGES_PER_STEP))


def _online_update(s, mask, v_dot, m_sc, l_sc, acc_sc):
    m_prev = m_sc[...]
    m_new = jnp.maximum(m_prev, jnp.max(s, axis=-1, keepdims=True))
    alpha = jnp.exp(m_prev - m_new)
    p = jnp.exp(s - m_new)
    if mask is not None:
        p = jnp.where(mask, p, 0.0)
    l_sc[...] = alpha * l_sc[...] + jnp.sum(p, axis=-1, keepdims=True)
    acc_sc[...] = alpha * acc_sc[...] + v_dot(p.astype(BF16))
    m_sc[...] = m_new


def _flash_init(j, m_sc, l_sc, acc_sc):
    @pl.when(j == 0)
    def _():
        m_sc[...] = jnp.full_like(m_sc, NEG_INF)
        l_sc[...] = jnp.zeros_like(l_sc)
        acc_sc[...] = jnp.zeros_like(acc_sc)


def _mla_paged_kernel(tbl_ref, q_ref, new_ref, *rest, n_steps, T, H, scale):
    pages = rest[:PAGES_PER_STEP]
    o_ref, m_sc, l_sc, acc_sc = rest[PAGES_PER_STEP:]
    j = pl.program_id(1)
    C = MLA_KV_LORA
    R = T * H
    _flash_init(j, m_sc, l_sc, acc_sc)
    qb = q_ref[...].astype(BF16)
    pg = [p_[...].astype(BF16) for p_ in pages]
    s = jnp.concatenate([jnp.dot(qb, x, preferred_element_type=F32) for x in pg], axis=1) * scale

    def v_dot(p):
        out = None
        for u, x in enumerate(pg):
            term = _nt_dot(p[:, u * LANE:(u + 1) * LANE], x[:C, :])
            out = term if out is None else out + term
        return out

    _online_update(s, None, v_dot, m_sc, l_sc, acc_sc)

    @pl.when(j == n_steps - 1)
    def _():
        new = new_ref[...].astype(BF16)
        rows = lax.broadcasted_iota(jnp.int32, (R, 1), 0)
        ok = lax.shift_right_logical(rows, int(math.log2(H))) >= lax.broadcasted_iota(jnp.int32, (1, T), 1)
        s_new = jnp.where(ok, _nt_dot(qb, new) * scale, NEG_INF)
        _online_update(s_new, ok, lambda p: jnp.dot(p, new[:, :C], preferred_element_type=F32), m_sc, l_sc, acc_sc)
        o_ref[...] = acc_sc[...] / jnp.maximum(l_sc[...], 1e-30)


def mla_paged(q_abs, new_rows, cache_t, table, layer, *, past_len, T):
    B, R, Dk = q_abs.shape
    n_pages = past_len // LANE
    n_steps = n_pages // PAGES_PER_STEP
    assert n_pages % PAGES_PER_STEP == 0
    kern = functools.partial(_mla_paged_kernel, n_steps=n_steps, T=T, H=MLA_HEADS,
                             scale=(MLA_NOPE + MLA_ROPE) ** -0.5)
    return pl.pallas_call(
        kern,
        out_shape=jax.ShapeDtypeStruct((B, R, MLA_KV_LORA), F32),
        grid_spec=pltpu.PrefetchScalarGridSpec(
            num_scalar_prefetch=1, grid=(B, n_steps),
            in_specs=[pl.BlockSpec((None, R, Dk), lambda b, j, tbl: (b, 0, 0)),
                      pl.BlockSpec((None, T, Dk), lambda b, j, tbl: (b, 0, 0))]
            + _page_specs(Dk, 0, layer, n_pages),
            out_specs=pl.BlockSpec((None, R, MLA_KV_LORA), lambda b, j, tbl: (b, 0, 0)),
            scratch_shapes=[pltpu.VMEM((R, 1), F32), pltpu.VMEM((R, 1), F32), pltpu.VMEM((R, MLA_KV_LORA), F32)]),
        compiler_params=_cparams(("parallel", "arbitrary")),
        name="mla_paged",
    )(table, q_abs, new_rows, *([cache_t] * PAGES_PER_STEP))


def _nsa_cmp_paged_kernel(tbl_ref, q_ref, w1_ref, bias_ref, w2_ref, ovn_ref, *rest, slopes, past_len, n_steps, T, H,
                          n_cmp, n_sel, scale):
    pages = rest[:PAGES_PER_STEP]
    o_ref, sel_ref, xt_sc, part_sc = rest[PAGES_PER_STEP:]
    j = pl.program_id(1)
    HD, FH = HEAD_DIM, NSA_CMP_HIDDEN
    blocks_per_step = PAGES_PER_STEP * LANE // NSA_CMP_STRIDE
    m_blocks = n_steps * blocks_per_step
    R = T * H

    @pl.when(j == 0)
    def _():
        part_sc[pl.ds(m_blocks, 8), :] = jnp.zeros((8, part_sc.shape[1]), F32)

    for u, p_ in enumerate(pages):
        xt_sc[pl.ds(u * LANE, LANE), :] = p_[...].T
    acc = jnp.zeros((blocks_per_step, part_sc.shape[1]), F32)
    for jj in range(NSA_CMP_STRIDE):
        x = xt_sc[pl.ds(jj, blocks_per_step, stride=NSA_CMP_STRIDE), :]
        acc = acc + jnp.dot(x.astype(BF16), w1_ref[jj], preferred_element_type=F32)
    part_sc[pl.ds(pl.multiple_of(j * blocks_per_step, blocks_per_step), blocks_per_step), :] = acc

    @pl.when(j == n_steps - 1)
    def _():
        comp = []
        for kk in range(2):
            a0 = part_sc[pl.ds(0, m_blocks), pl.ds(kk * 2 * FH, FH)]
            a1 = part_sc[pl.ds(1, m_blocks), pl.ds(kk * 2 * FH + FH, FH)]
            act = jax.nn.gelu(a0 + a1 + bias_ref[:, pl.ds(kk * FH, FH)])
            comp.append(jnp.dot(act.astype(BF16), w2_ref[kk].astype(BF16), preferred_element_type=F32))
        qb = q_ref[...].astype(BF16)
        rows = lax.broadcasted_iota(jnp.int32, (R, 1), 0)
        shift = int(math.log2(H))
        qpos = past_len + lax.shift_right_logical(rows, shift)
        n_idx = lax.broadcasted_iota(jnp.int32, (1, m_blocks), 1)
        dist = qpos - (n_idx * NSA_CMP_STRIDE + (NSA_CMP_LEN - 1))
        mask = jnp.logical_and(dist >= 0, n_idx < n_cmp)
        slope = _row_slopes(jnp.bitwise_and(rows, H - 1), slopes, (R, 1))
        s = _nt_dot(qb, comp[0].astype(BF16)) * scale - slope * dist.astype(F32)
        s = jnp.where(mask, s, NEG_INF)
        e = jnp.where(mask, jnp.exp(s - jnp.max(s, axis=-1, keepdims=True)), 0.0)
        p = e / jnp.maximum(jnp.sum(e, axis=-1, keepdims=True), 1e-30)
        o_ref[...] = jnp.dot(p.astype(BF16), comp[1].astype(BF16), preferred_element_type=F32)
        head_sum = (lax.shift_right_logical(lax.broadcasted_iota(jnp.int32, (T, R), 1), shift)
                    == lax.broadcasted_iota(jnp.int32, (T, R), 0)).astype(F32)
        p_sum = jnp.dot(head_sum, p, preferred_element_type=F32, precision=HIGHEST)
        imp = jnp.dot(p_sum, ovn_ref[...], preferred_element_type=F32, precision=HIGHEST)
        nsp = imp.shape[1]
        lane = lax.broadcasted_iota(jnp.int32, (1, nsp), 1)
        own = lax.shift_right_logical(past_len + lax.broadcasted_iota(jnp.int32, (T, 1), 0),
                                      int(math.log2(NSA_SEL_BLOCK)))
        forced = jnp.logical_or(lane == 0, jnp.logical_or(lane == own, lane == own - 1))
        valid = jnp.logical_and(lane <= own, lane < n_sel)
        score = jnp.where(valid, imp + jnp.where(forced, FORCE_BONUS, 0.0), NEG_INF)
        rank = jnp.zeros((T, nsp), jnp.int32)
        for c2 in range(n_sel):
            col = score[:, c2:c2 + 1]
            beats = jnp.logical_or(col > score, jnp.logical_and(col == score, lane > c2))
            rank = rank + beats.astype(jnp.int32)
        sel_ref[...] = jnp.logical_and(valid, rank < NSA_TOPN).astype(F32)


def nsa_cmp_paged(q, w1bd, bias2, w2, ovn, cache_t, table, layer, *, past_len, T, n_cmp, n_sel):
    B, R, HD = q.shape
    n_pages = past_len // LANE
    n_steps = n_pages // PAGES_PER_STEP
    assert n_pages % PAGES_PER_STEP == 0
    m_blocks = past_len // NSA_CMP_STRIDE
    nsp = ovn.shape[1]
    pw = w1bd.shape[2]
    kern = functools.partial(_nsa_cmp_paged_kernel, slopes=_alibi_slopes(NSA_HEADS), past_len=past_len,
                             n_steps=n_steps, T=T, H=NSA_HEADS, n_cmp=n_cmp, n_sel=n_sel, scale=HD ** -0.5)
    return pl.pallas_call(
        kern,
        out_shape=(jax.ShapeDtypeStruct((B, R, HD), F32), jax.ShapeDtypeStruct((B, T, nsp), F32)),
        grid_spec=pltpu.PrefetchScalarGridSpec(
            num_scalar_prefetch=1, grid=(B, n_steps),
            in_specs=[pl.BlockSpec((None, R, HD), lambda b, j, tbl: (b, 0, 0)),
                      pl.BlockSpec(w1bd.shape, lambda b, j, tbl: (0, 0, 0)),
                      pl.BlockSpec(bias2.shape, lambda b, j, tbl: (0, 0)),
                      pl.BlockSpec(w2.shape, lambda b, j, tbl: (0, 0, 0)),
                      pl.BlockSpec(ovn.shape, lambda b, j, tbl: (0, 0))]
            + _page_specs(2 * HD, 0, layer, n_pages),
            out_specs=(pl.BlockSpec((None, R, HD), lambda b, j, tbl: (b, 0, 0)),
                       pl.BlockSpec((None, T, nsp), lambda b, j, tbl: (b, 0, 0))),
            scratch_shapes=[pltpu.VMEM((PAGES_PER_STEP * LANE, 2 * HD), F32), pltpu.VMEM((m_blocks + 8, pw), F32)]),
        compiler_params=_cparams(("parallel", "arbitrary")),
        name="nsa_cmp_paged",
    )(table, q, w1bd, bias2, w2, ovn, *([cache_t] * PAGES_PER_STEP))


def _nsa_sel_paged_kernel(tbl_ref, q_ref, selx_ref, selown_ref, knew_ref, wnew_ref, win_ref, *rest, slopes,
                          past_len, n_steps, T, H, scale):
    pages = rest[:PAGES_PER_STEP]
    osel_ref, owin_ref, m_sc, l_sc, acc_sc = rest[PAGES_PER_STEP:]
    j = pl.program_id(1)
    HD = HEAD_DIM
    R = T * H
    K = PAGES_PER_STEP * LANE
    blocks_per_step = K // NSA_SEL_BLOCK
    _flash_init(j, m_sc, l_sc, acc_sc)
    rows = lax.broadcasted_iota(jnp.int32, (R, 1), 0)
    t_of_row = lax.shift_right_logical(rows, int(math.log2(H)))
    qpos = past_len + t_of_row
    slope = _row_slopes(jnp.bitwise_and(rows, H - 1), slopes, (R, 1))
    qb = q_ref[...].astype(BF16)

    pg = [p_[...].astype(BF16) for p_ in pages]
    s = jnp.concatenate([jnp.dot(qb, x[:HD, :], preferred_element_type=F32) for x in pg], axis=1) * scale
    kidx = lax.broadcasted_iota(jnp.int32, (1, K), 1)
    s = s - slope * (qpos - (j * K + kidx)).astype(F32)
    expand = (lax.broadcasted_iota(jnp.int32, (blocks_per_step, K), 0)
              == lax.shift_right_logical(lax.broadcasted_iota(jnp.int32, (blocks_per_step, K), 1),
                                         int(math.log2(NSA_SEL_BLOCK)))).astype(BF16)
    picked = jnp.dot(selx_ref[...].astype(BF16), expand, preferred_element_type=F32) > 0.5
    s = jnp.where(picked, s, NEG_INF)

    def v_dot(p):
        out = None
        for u, x in enumerate(pg):
            term = _nt_dot(p[:, u * LANE:(u + 1) * LANE], x[HD:, :])
            out = term if out is None else out + term
        return out

    _online_update(s, picked, v_dot, m_sc, l_sc, acc_sc)

    dist_new = t_of_row - lax.broadcasted_iota(jnp.int32, (1, T), 1)

    @pl.when(j == 0)
    def _():
        wb = win_ref.shape[1]
        win = win_ref[...].astype(BF16)
        wnew = wnew_ref[...].astype(BF16)
        dist_w = (qpos - (past_len - wb)) - lax.broadcasted_iota(jnp.int32, (1, wb), 1)
        ok_w = jnp.logical_and(dist_w >= 0, dist_w < NSA_WINDOW)
        ok_n = jnp.logical_and(dist_new >= 0, dist_new < NSA_WINDOW)
        s_w = jnp.dot(qb, win[:HD, :], preferred_element_type=F32) * scale - slope * dist_w.astype(F32)
        s_n = _nt_dot(qb, wnew[:, :HD]) * scale - slope * dist_new.astype(F32)
        s_w = jnp.where(ok_w, s_w, NEG_INF)
        s_n = jnp.where(ok_n, s_n, NEG_INF)
        m = jnp.maximum(jnp.max(s_w, axis=-1, keepdims=True), jnp.max(s_n, axis=-1, keepdims=True))
        p_w = jnp.where(ok_w, jnp.exp(s_w - m), 0.0)
        p_n = jnp.where(ok_n, jnp.exp(s_n - m), 0.0)
        l = jnp.sum(p_w, axis=-1, keepdims=True) + jnp.sum(p_n, axis=-1, keepdims=True)
        o = _nt_dot(p_w.astype(BF16), win[HD:, :]) + jnp.dot(p_n.astype(BF16), wnew[:, HD:],
                                                             preferred_element_type=F32)
        owin_ref[...] = o / jnp.maximum(l, 1e-30)

    @pl.when(j == n_steps - 1)
    def _():
        knew = knew_ref[...].astype(BF16)
        ok = jnp.logical_and(dist_new >= 0, selown_ref[...] > 0.5)
        s_new = _nt_dot(qb, knew[:, :HD]) * scale - slope * dist_new.astype(F32)
        s_new = jnp.where(ok, s_new, NEG_INF)
        _online_update(s_new, ok, lambda p: jnp.dot(p, knew[:, HD:], preferred_element_type=F32), m_sc, l_sc, acc_sc)
        osel_ref[...] = acc_sc[...] / jnp.maximum(l_sc[...], 1e-30)


def nsa_sel_paged(q, selx, selown, proj, win_t, cache_t, table, layer, *, past_len, T):
    B, R, HD = q.shape
    n_pages = past_len // LANE
    n_steps = n_pages // PAGES_PER_STEP
    wb = win_t.shape[3]
    bps = selx.shape[3]
    kern = functools.partial(_nsa_sel_paged_kernel, slopes=_alibi_slopes(NSA_HEADS), past_len=past_len,
                             n_steps=n_steps, T=T, H=NSA_HEADS, scale=HD ** -0.5)
    out = jax.ShapeDtypeStruct((B, R, HD), F32)
    o_spec = pl.BlockSpec((None, R, HD), lambda b, j, tbl: (b, 0, 0))
    return pl.pallas_call(
        kern,
        out_shape=(out, out),
        grid_spec=pltpu.PrefetchScalarGridSpec(
            num_scalar_prefetch=1, grid=(B, n_steps),
            in_specs=[pl.BlockSpec((None, R, HD), lambda b, j, tbl: (b, 0, 0)),
                      pl.BlockSpec((None, None, R, bps), lambda b, j, tbl: (b, j, 0, 0)),
                      pl.BlockSpec((None, R, 1), lambda b, j, tbl: (b, 0, 0)),
                      pl.BlockSpec((T, 2 * HD), lambda b, j, tbl: (b, (C_NKV + 2 * HD) // (2 * HD))),
                      pl.BlockSpec((T, 2 * HD), lambda b, j, tbl: (b, (C_NKV + 4 * HD) // (2 * HD))),
                      pl.BlockSpec((None, None, 2 * HD, wb), lambda b, j, tbl: (layer, b, 0, 0))]
            + _page_specs(2 * HD, 1, layer, n_pages),
            out_specs=(o_spec, o_spec),
            scratch_shapes=[pltpu.VMEM((R, 1), F32), pltpu.VMEM((R, 1), F32), pltpu.VMEM((R, HD), F32)]),
        compiler_params=_cparams(("parallel", "arbitrary")),
        name="nsa_sel_paged",
    )(table, q, selx, selown, proj, proj, win_t, *([cache_t] * PAGES_PER_STEP))


def _merge_kernel(oa_ref, ob_ref, oc_ref, wb_ref, ga_ref, gb_ref, gc_ref, o_ref):
    acc = None
    for n, (o_r, g_r) in enumerate(((oa_ref, ga_ref), (ob_ref, gb_ref), (oc_ref, gc_ref))):
        up = jnp.dot(o_r[...].astype(BF16), wb_ref[n], preferred_element_type=F32)
        term = jax.nn.sigmoid(g_r[...]) * up
        acc = term if acc is None else acc + term
    o_ref[...] = acc.astype(o_ref.dtype)


def merge_branches(o_a, o_b, o_c, w_branch, proj, d_model, *, tm=512, tn=512):
    M = o_a.shape[0]
    tm = _pick_tile(M, tm)
    tn = _pick_tile(d_model, tn)
    nj = d_model // tn
    goff = C_MGATE // tn

    def gate_spec(n):
        return pl.BlockSpec((tm, tn), lambda i, j: (i, goff + n * nj + j))

    o_spec = pl.BlockSpec((tm, BRANCH_WIDTH), lambda i, j: (i, 0))
    return pl.pallas_call(
        _merge_kernel,
        out_shape=jax.ShapeDtypeStruct((M, d_model), BF16),
        grid=(M // tm, nj),
        in_specs=[o_spec, o_spec, o_spec,
                  pl.BlockSpec((N_BRANCH, BRANCH_WIDTH, tn), lambda i, j: (0, 0, j)),
                  gate_spec(0), gate_spec(1), gate_spec(2)],
        out_specs=pl.BlockSpec((tm, tn), lambda i, j: (i, j)),
        compiler_params=_cparams(("parallel", "arbitrary")),
        name="merge_branches",
    )(o_a, o_b, o_c, w_branch, proj, proj, proj)


def _layer_norm_rows(v, g, b):
    mu = jnp.mean(v, axis=-1, keepdims=True)
    var = jnp.mean(jnp.square(v - mu), axis=-1, keepdims=True)
    return (v - mu) * lax.rsqrt(var + LN_EPS) * g + b


def _out_ln_kernel(m_ref, w_ref, x_ref, g_ref, b_ref, o_ref, *, alpha):
    mix = jnp.dot(m_ref[...].astype(BF16), w_ref[...], preferred_element_type=F32)
    o_ref[...] = _layer_norm_rows(alpha * x_ref[...] + mix, g_ref[...], b_ref[...])


def out_proj_ln(merged, w_out, x, g, b, *, alpha, tm=256):
    M, D = x.shape
    tm = _pick_tile(M, tm)
    return pl.pallas_call(
        functools.partial(_out_ln_kernel, alpha=alpha),
        out_shape=jax.ShapeDtypeStruct((M, D), F32),
        grid=(M // tm,),
        in_specs=[pl.BlockSpec((tm, D), lambda i: (i, 0)),
                  pl.BlockSpec((D, D), lambda i: (0, 0)),
                  pl.BlockSpec((tm, D), lambda i: (i, 0)),
                  pl.BlockSpec((1, D), lambda i: (0, 0)),
                  pl.BlockSpec((1, D), lambda i: (0, 0))],
        out_specs=pl.BlockSpec((tm, D), lambda i: (i, 0)),
        compiler_params=_cparams(("parallel",)),
        name="out_proj_ln1",
    )(merged, w_out, x, g, b)


def _moe_up_kernel(te_ref, nt_ref, xs_ref, wg_ref, wu_ref, rw_ref, o_ref):
    i = pl.program_id(0)

    @pl.when(i < nt_ref[0])
    def _():
        x = xs_ref[...]
        gate = jnp.dot(x, wg_ref[...], preferred_element_type=F32)
        up = jnp.dot(x, wu_ref[...], preferred_element_type=F32)
        o_ref[...] = (jax.nn.silu(gate) * up * rw_ref[...]).astype(o_ref.dtype)

    @pl.when(i >= nt_ref[0])
    def _():
        o_ref[...] = jnp.zeros_like(o_ref)


def _moe_down_kernel(te_ref, nt_ref, h_ref, wd_ref, o_ref):
    i = pl.program_id(0)

    @pl.when(i < nt_ref[0])
    def _():
        o_ref[...] = jnp.dot(h_ref[...], wd_ref[...], preferred_element_type=F32)

    @pl.when(i >= nt_ref[0])
    def _():
        o_ref[...] = jnp.zeros_like(o_ref)


def moe_experts(xs, row_w, tile_expert, n_tiles, w_gate, w_up, w_down, *, tm):
    P, D = xs.shape
    Fe = w_gate.shape[2]
    nt = P // tm
    hidden = pl.pallas_call(
        _moe_up_kernel,
        out_shape=jax.ShapeDtypeStruct((P, Fe), BF16),
        grid_spec=pltpu.PrefetchScalarGridSpec(
            num_scalar_prefetch=2, grid=(nt,),
            in_specs=[pl.BlockSpec((tm, D), lambda i, te, n: (i, 0)),
                      pl.BlockSpec((None, D, Fe), lambda i, te, n: (te[i], 0, 0)),
                      pl.BlockSpec((None, D, Fe), lambda i, te, n: (te[i], 0, 0)),
                      pl.BlockSpec((tm, 1), lambda i, te, n: (i, 0))],
            out_specs=pl.BlockSpec((tm, Fe), lambda i, te, n: (i, 0))),
        compiler_params=_cparams(("arbitrary",)),
        name="moe_up",
    )(tile_expert, n_tiles, xs, w_gate, w_up, row_w)
    return pl.pallas_call(
        _moe_down_kernel,
        out_shape=jax.ShapeDtypeStruct((P, D), F32),
        grid_spec=pltpu.PrefetchScalarGridSpec(
            num_scalar_prefetch=2, grid=(nt,),
            in_specs=[pl.BlockSpec((tm, Fe), lambda i, te, n: (i, 0)),
                      pl.BlockSpec((None, Fe, D), lambda i, te, n: (te[i], 0, 0))],
            out_specs=pl.BlockSpec((tm, D), lambda i, te, n: (i, 0))),
        compiler_params=_cparams(("arbitrary",)),
        name="moe_down",
    )(tile_expert, n_tiles, hidden, w_down)


def _ffn_ln_kernel(h_ref, f_ref, p_ref, wg_ref, wp_ref, g_ref, b_ref, o_ref, *, alpha):
    h = h_ref[...]
    gate = jax.nn.sigmoid(jnp.dot(h.astype(BF16), wg_ref[...], preferred_element_type=F32))
    ple = gate * jnp.dot(p_ref[...].astype(BF16), wp_ref[...], preferred_element_type=F32)
    o_ref[...] = _layer_norm_rows(alpha * h + f_ref[...] + ple, g_ref[...], b_ref[...])


def ffn_ple_ln(h, ffn, p, w_pg, w_pp, g, b, *, alpha, tm=256):
    M, D = h.shape
    Pd = p.shape[1]
    tm = _pick_tile(M, tm)
    return pl.pallas_call(
        functools.partial(_ffn_ln_kernel, alpha=alpha),
        out_shape=jax.ShapeDtypeStruct((M, D), F32),
        grid=(M // tm,),
        in_specs=[pl.BlockSpec((tm, D), lambda i: (i, 0)),
                  pl.BlockSpec((tm, D), lambda i: (i, 0)),
                  pl.BlockSpec((tm, Pd), lambda i: (i, 0)),
                  pl.BlockSpec((D, D), lambda i: (0, 0)),
                  pl.BlockSpec((Pd, D), lambda i: (0, 0)),
                  pl.BlockSpec((1, D), lambda i: (0, 0)),
                  pl.BlockSpec((1, D), lambda i: (0, 0))],
        out_specs=pl.BlockSpec((tm, D), lambda i: (i, 0)),
        compiler_params=_cparams(("parallel",)),
        name="ffn_ple_ln2",
    )(h, ffn, p, w_pg, w_pp, g, b)


def _alibi_slopes(n):
    return tuple(np.asarray(2.0 ** (-8.0 * np.arange(1, n + 1) / n), dtype=np.float32).tolist())


def _rms_norm(x, g):
    return x * lax.rsqrt(jnp.square(x).mean(-1, keepdims=True) + RMS_EPS) * g


def _rope(x, pos):
    half = x.shape[-1] // 2
    freqs = ROPE_BASE ** (-jnp.arange(half, dtype=F32) / half)
    ang = pos.astype(F32)[:, None] * freqs[None, :]
    shape = (pos.shape[0],) + (1,) * (x.ndim - 3) + (half,)
    cos = jnp.cos(ang).reshape(shape)
    sin = jnp.sin(ang).reshape(shape)
    x1, x2 = x[..., :half], x[..., half:]
    return jnp.concatenate([x1 * cos - x2 * sin, x1 * sin + x2 * cos], -1)


def _pad_axis(x, axis, size):
    if x.shape[axis] == size:
        return x
    pad = [(0, 0)] * x.ndim
    pad[axis] = (0, size - x.shape[axis])
    return jnp.pad(x, pad)


def _topk_mask(score, k):
    n = score.shape[-1]
    mine = score[..., :, None]
    other = score[..., None, :]
    idx = jnp.arange(n)
    beats = (other > mine) | ((other == mine) & (idx[None, :] < idx[:, None]))
    return beats.sum(-1) < k


def _overlap_weights(m, n_cmp, n_sel, nsp):
    cst = np.arange(m) * NSA_CMP_STRIDE
    sst = np.arange(n_sel) * NSA_SEL_BLOCK
    ov = np.clip(np.minimum(cst[:, None] + NSA_CMP_LEN, sst[None, :] + NSA_SEL_BLOCK)
                 - np.maximum(cst[:, None], sst[None, :]), 0, None)
    ovn = np.zeros((m, nsp), np.float32)
    ovn[:n_cmp, :n_sel] = ov[:n_cmp].astype(np.float32) / NSA_CMP_LEN
    return jnp.asarray(ovn)


def _cmp_bias(pe, w1cat):
    out = []
    for kk in range(2):
        pe_part = mm(pe[kk], w1cat[kk], tn=2 * NSA_CMP_HIDDEN, name="nsa_cmp_pos")
        out.append(pe_part[0, :NSA_CMP_HIDDEN] + pe_part[1, NSA_CMP_HIDDEN:])
    return jnp.stack(out)


def moba_attention(mq, k_full, v_full, q_pos0):
    B, T, _ = mq.shape
    H, KVH, HD = MOBA_HEADS, MOBA_KV_HEADS, HEAD_DIM
    G = H // KVH
    L = k_full.shape[1]
    nb = -(-L // MOBA_BLOCK)
    Lp = nb * MOBA_BLOCK
    kf = _pad_axis(k_full, 1, Lp).transpose(0, 2, 1, 3).reshape(B * KVH, Lp, HD)
    vf = _pad_axis(v_full, 1, Lp).transpose(0, 2, 1, 3).reshape(B * KVH, Lp, HD)
    q = mq.reshape(B, T, KVH, G, HD).transpose(0, 2, 1, 3, 4).reshape(B * KVH, T * G, HD)
    k_mean = block_mean(kf, MOBA_BLOCK)
    nbp = _round_up(nb, LANE)
    gate = bmm_f32(q, _pad_axis(k_mean.transpose(0, 2, 1), 2, nbp), name="moba_gate")[..., :nb]
    q_pos = q_pos0 + jnp.arange(T, dtype=jnp.int32)
    own = jnp.repeat(q_pos // MOBA_BLOCK, G)
    past_blk = jnp.arange(nb)[None, :] < own[:, None]
    gate = jnp.where(past_blk[None], gate, NEG_INF)
    picked = (_topk_mask(gate, min(MOBA_TOPK, nb)) & past_blk[None]) | (jnp.arange(nb)[None, :] == own[:, None])[None]
    sel = _pad_axis(picked.astype(F32), 2, nbp)
    out = flash(q, kf, vf, G=G, slopes=_alibi_slopes(H), kv_heads=KVH, qpos0=q_pos0, lk=L, sel=sel,
                sel_blk=MOBA_BLOCK, scale=HD ** -0.5, tr=1024, tk=MOBA_BLOCK, name="moba_attn")
    return out.reshape(B, KVH, T, G, HD).transpose(0, 2, 1, 3, 4).reshape(B, T, H * HD)


def nsa_attention(nq, ngate, cmp_k, cmp_v, sel_k, sel_v, win_k, win_v, win_pos0, q_pos0, T, cw):
    B = nq.shape[0]
    H, HD = NSA_HEADS, HEAD_DIM
    slopes = _alibi_slopes(H)
    scale = HD ** -0.5
    L = cmp_k.shape[1]
    q = nq.reshape(B, T * H, HD)
    m = L // NSA_CMP_STRIDE
    n_cmp = m - NSA_CMP_LEN // NSA_CMP_STRIDE + 1
    comp = []
    for kk, rows in enumerate((cmp_k, cmp_v)):
        sub = rows[:, :m * NSA_CMP_STRIDE].reshape(B * m, NSA_CMP_STRIDE * HD)
        part = mm(sub, cw['w1cat'][kk], tm=1024, tn=2 * NSA_CMP_HIDDEN, name="nsa_cmp_w1")
        part = part.reshape(B, m, 2 * NSA_CMP_HIDDEN)
        a0 = part[:, :, :NSA_CMP_HIDDEN]
        a1 = jnp.concatenate([part[:, 1:, NSA_CMP_HIDDEN:], jnp.zeros((B, 1, NSA_CMP_HIDDEN), F32)], axis=1)
        c = gelu_proj(a0.reshape(B * m, -1), a1.reshape(B * m, -1), cw['bias'][kk][None, :], cw['w2'][kk])
        comp.append(c.reshape(B, m, HD))
    n_sel = -(-L // NSA_SEL_BLOCK)
    nsp = _round_up(n_sel, LANE)
    o_cmp, imp = cmp_attention(q, comp[0], comp[1], _overlap_weights(m, n_cmp, n_sel, nsp), H=H, slopes=slopes,
                               qpos0=q_pos0, n_cmp=n_cmp, scale=scale, tr=512)
    imp = imp[..., :n_sel]
    q_pos = q_pos0 + jnp.arange(T, dtype=jnp.int32)
    own = q_pos // NSA_SEL_BLOCK
    jj = jnp.arange(n_sel)[None, :]
    forced = (jj == 0) | (jj == own[:, None]) | (jj == own[:, None] - 1)
    valid = jj <= own[:, None]
    score = jnp.where(valid[None], imp + jnp.where(forced[None], FORCE_BONUS, 0.0), NEG_INF)
    sel = _topk_mask(score, min(NSA_TOPN, n_sel)) & valid[None]
    sel = _pad_axis(sel.astype(F32), 2, nsp)
    sel = jnp.repeat(sel, H, axis=1)
    tk = 512
    Lp = _round_up(L, tk)
    o_sel = flash(q, _pad_axis(sel_k, 1, Lp), _pad_axis(sel_v, 1, Lp), G=H, slopes=slopes, qpos0=q_pos0, lk=L,
                  sel=sel, sel_blk=NSA_SEL_BLOCK, scale=scale, tr=1024, tk=tk, name="nsa_sel_attn")
    Lw = win_k.shape[1]
    tkw = 512 if Lw >= 512 else _round_up(Lw, LANE)
    Lwp = _round_up(Lw, tkw)
    o_win = flash(q, _pad_axis(win_k, 1, Lwp), _pad_axis(win_v, 1, Lwp), G=H, slopes=slopes, qpos0=q_pos0,
                  kpos0=win_pos0, lk=Lw, window=NSA_WINDOW, scale=scale, tr=1024, tk=tkw, name="nsa_win_attn")
    g = jax.nn.sigmoid(ngate.reshape(B, T, 3, H))[..., None]
    o = (g[:, :, 0] * o_cmp.reshape(B, T, H, HD) + g[:, :, 1] * o_sel.reshape(B, T, H, HD)
         + g[:, :, 2] * o_win.reshape(B, T, H, HD))
    return o.reshape(B, T, H * HD)


def moba_attention_paged(mq, proj, past, T):
    B = mq.shape[0]
    H, KVH, HD = MOBA_HEADS, MOBA_KV_HEADS, HEAD_DIM
    G = H // KVH
    q = mq.reshape(B, T, KVH, G, HD).transpose(0, 2, 1, 3, 4).reshape(B, KVH * T * G, HD)
    out = moba_paged(q, proj, past['moba_t'], past['table'], past['layer'], past_len=past['len'], T=T)
    return out.reshape(B, KVH, T, G, HD).transpose(0, 2, 1, 3, 4).reshape(B, T, H * HD)


def nsa_attention_paged(nq, ngate, proj, past, T, cw):
    B = nq.shape[0]
    H, HD = NSA_HEADS, HEAD_DIM
    past_len = past['len']
    L = past_len + T
    m = L // NSA_CMP_STRIDE
    assert m == past_len // NSA_CMP_STRIDE and past_len % NSA_SEL_BLOCK == 0 and T <= NSA_SEL_BLOCK
    n_cmp = m - NSA_CMP_LEN // NSA_CMP_STRIDE + 1
    n_sel = -(-L // NSA_SEL_BLOCK)
    nsp = _round_up(n_sel, LANE)
    q = nq.reshape(B, T * H, HD)
    o_cmp, sel = nsa_cmp_paged(q, cw['w1bd'], cw['bias'].reshape(1, -1), cw['w2'], _overlap_weights(m, n_cmp, n_sel, nsp),
                               past['nsa_t'], past['table'], past['layer'], past_len=past_len, T=T, n_cmp=n_cmp,
                               n_sel=n_sel)
    n_past_blk = past_len // NSA_SEL_BLOCK
    bps = PAGES_PER_STEP * LANE // NSA_SEL_BLOCK
    selx = sel[:, :, :n_past_blk].reshape(B, T, n_past_blk // bps, bps).transpose(0, 2, 1, 3)
    selx = jnp.repeat(selx, H, axis=2)
    selown = jnp.repeat(sel[:, :, n_past_blk], H, axis=1)[..., None]
    o_sel, o_win = nsa_sel_paged(q, selx, selown, proj, past['win_t'], past['nsa_t'], past['table'], past['layer'],
                                 past_len=past_len, T=T)
    g = jax.nn.sigmoid(ngate.reshape(B, T, 3, H))[..., None]
    o = (g[:, :, 0] * o_cmp.reshape(B, T, H, HD) + g[:, :, 1] * o_sel.reshape(B, T, H, HD)
         + g[:, :, 2] * o_win.reshape(B, T, H, HD))
    return o.reshape(B, T, H * HD)


def mla_attention(q_abs, lat_full, q_pos0, T):
    L = lat_full.shape[1]
    tk = 512
    Lp = _round_up(L, tk)
    kf = _pad_axis(lat_full, 1, Lp)
    return flash(q_abs, kf, kf[..., :MLA_KV_LORA], G=MLA_HEADS, slopes=None, qpos0=q_pos0, lk=L,
                 scale=(MLA_NOPE + MLA_ROPE) ** -0.5, tr=1024, tk=tk, name="mla_attn")


def token_mixer(x2, B, T, q_pos0, past, lw):
    N = B * T
    pos = q_pos0 + jnp.arange(T, dtype=jnp.int32)
    proj = mm(x2, lw['w_in'], tm=1024, tn=512, name="in_proj")
    mq = proj[:, C_MQ:C_MQ + 512].reshape(B, T, 512)
    moba_rows = proj[:, C_MKV:C_MKV + 256].reshape(B, T, MOBA_KV_HEADS, 2, HEAD_DIM)
    nq = proj[:, C_NQ:C_NQ + 512].reshape(B, T, 512)
    nsa_kv = proj[:, C_NKV:C_NKV + 384].reshape(B, T, 6, HEAD_DIM)
    ngate = proj[:, C_NGATE:C_NGATE + 3 * NSA_HEADS]
    qa = proj[:, C_QA:C_QA + MLA_Q_LORA]
    kva = proj[:, C_KVA:C_KVA + MLA_KV_LORA + MLA_ROPE].reshape(B, T, -1)
    nsa_rows, win_rows = nsa_kv[:, :, :4], nsa_kv[:, :, 4:]
    mla_rows = jnp.concatenate([_rms_norm(kva[..., :MLA_KV_LORA], lw['mla_kv_norm']),
                                _rope(kva[..., MLA_KV_LORA:], pos)], -1)
    q_mla = mm(_rms_norm(qa, lw['mla_q_norm']), lw['mla_w_qb'], name="mla_q_up")
    q_mla = q_mla.reshape(B, T, MLA_HEADS, MLA_NOPE + MLA_ROPE)
    q_lat = mm(q_mla[..., :MLA_NOPE].reshape(N, -1), lw['mla_w_uk_bd'], name="mla_q_absorb")
    q_rope = _rope(q_mla[..., MLA_NOPE:], pos)
    q_abs = jnp.concatenate([q_lat.reshape(B, T, MLA_HEADS, MLA_KV_LORA), q_rope], -1)
    q_abs = q_abs.reshape(B, T * MLA_HEADS, MLA_KV_LORA + MLA_ROPE)

    if past is None:
        new_win = win_rows[:, -min(NSA_WINDOW, T):]
        o_a = moba_attention(mq, moba_rows[:, :, :, 0], moba_rows[:, :, :, 1], q_pos0)
        o_b = nsa_attention(nq, ngate, nsa_rows[:, :, 0], nsa_rows[:, :, 1], nsa_rows[:, :, 2], nsa_rows[:, :, 3],
                            win_rows[:, :, 0], win_rows[:, :, 1], 0, q_pos0, T, lw['cmp'])
        o_lat = mla_attention(q_abs, mla_rows, q_pos0, T)
    else:
        new_win = jnp.concatenate([past['win'][:, T:], win_rows], axis=1)
        o_a = moba_attention_paged(mq, proj, past, T)
        o_b = nsa_attention_paged(nq, ngate, proj, past, T, lw['cmp'])
        o_lat = mla_paged(q_abs, mla_rows, past['mla_t'], past['table'], past['layer'], past_len=q_pos0, T=T)
    o_c = mm(o_lat.reshape(N, MLA_HEADS * MLA_KV_LORA), lw['mla_w_uv_bd'], name="mla_v_up")
    d_model = x2.shape[1]
    merged = merge_branches(o_a.reshape(N, -1), o_b.reshape(N, -1), o_c, lw['w_branch'], proj, d_model)
    return merged, moba_rows, nsa_rows, mla_rows, new_win


def moe_ffn(h, router_w_pad, router_bias, lw, *, tm):
    n, D = h.shape
    per_group = N_EXPERTS // N_GROUPS
    logits = mm(h, router_w_pad, precision=HIGHEST, name="router")[:, :N_EXPERTS]
    aff = jax.nn.sigmoid(logits)
    biased = aff + router_bias
    group_score = lax.top_k(biased.reshape(n, N_GROUPS, per_group), TOPK_EXPERTS)[0].sum(-1)
    _, group_sel = lax.top_k(group_score, 1)
    in_group = (jnp.arange(N_EXPERTS) // per_group)[None, :] == group_sel
    _, expert_idx = lax.top_k(jnp.where(in_group, biased, NEG_INF), TOPK_EXPERTS)
    w = jnp.take_along_axis(aff, expert_idx, axis=-1)
    w = w / w.sum(-1, keepdims=True)
    A = n * TOPK_EXPERTS
    e_flat = expert_idx.reshape(A).astype(jnp.int32)
    w_flat = w.reshape(A)
    onehot = (e_flat[:, None] == jnp.arange(N_EXPERTS, dtype=jnp.int32)[None, :]).astype(jnp.int32)
    running = jnp.cumsum(onehot, axis=0)
    rank = jnp.take_along_axis(running, e_flat[:, None], axis=1)[:, 0] - 1
    counts = running[-1]
    padded = -(-counts // tm) * tm
    pend = jnp.cumsum(padded)
    pstart = pend - padded
    dest = pstart[e_flat] + rank
    P = _round_up(A, tm) + N_EXPERTS * tm
    slot_token = jnp.zeros((P,), jnp.int32).at[dest].set(jnp.arange(A, dtype=jnp.int32) // TOPK_EXPERTS)
    slot_w = jnp.zeros((P,), F32).at[dest].set(w_flat)
    pos = dest.reshape(n, TOPK_EXPERTS)
    tile_start = jnp.arange(P // tm, dtype=jnp.int32) * tm
    tile_expert = jnp.minimum((pend[None, :] <= tile_start[:, None]).sum(-1), N_EXPERTS - 1).astype(jnp.int32)
    n_tiles = (pend[-1] // tm).astype(jnp.int32).reshape(1)
    xs = h.astype(BF16)[slot_token]
    y = moe_experts(xs, slot_w[:, None], tile_expert, n_tiles, lw['moe_w_gate'], lw['moe_w_up'], lw['moe_w_down'],
                    tm=tm)
    return y[pos[:, 0]] + y[pos[:, 1]]


def decoder_layer(x2, p2, B, T, q_pos0, past, lw, router_w_pad, router_bias, alpha, moe_tm):
    merged, moba_r, nsa_r, mla_r, win_s = token_mixer(x2, B, T, q_pos0, past, lw)
    h = out_proj_ln(merged, lw['w_out'], x2, lw['ln1_g'], lw['ln1_b'], alpha=alpha)
    ffn = moe_ffn(h, router_w_pad, router_bias, lw, tm=moe_tm)
    y = ffn_ple_ln(h, ffn, p2, lw['ple_w_gate'], lw['ple_w_proj'], lw['ln2_g'], lw['ln2_b'], alpha=alpha)
    return y, (moba_r, nsa_r, mla_r, win_s)


def _pack_w_in(w_in, d_model):
    sizes = (512, 256, 512, 384, 3 * NSA_HEADS, MLA_Q_LORA, MLA_KV_LORA + MLA_ROPE, N_BRANCH * d_model)
    starts = (C_MQ, C_MKV, C_NQ, C_NKV, C_NGATE, C_QA, C_KVA, C_MGATE)
    total = C_MGATE + N_BRANCH * d_model
    total = _round_up(total, 512)
    out = jnp.zeros(w_in.shape[:2] + (total,), BF16)
    off = 0
    for s, c in zip(sizes, starts):
        out = lax.dynamic_update_slice_in_dim(out, w_in[..., off:off + s].astype(BF16), c, axis=2)
        off += s
    return out


def kernel(x_prompt, x_sample, cache_moba, cache_nsa, cache_mla, state_nsa_win, page_table, p_prompt, p_sample,
           w_in, mla_q_norm, mla_w_qb, mla_kv_norm, mla_w_uk, mla_w_uv, nsa_cmp_pos, nsa_cmp_w1, nsa_cmp_w2,
           w_branch, w_out, ln1_g, ln1_b, router_w, router_bias, moe_w_gate, moe_w_up, moe_w_down,
           ple_w_proj, ple_w_gate, ln2_g, ln2_b):
    depth = w_in.shape[0]
    Bp, Tp, D = x_prompt.shape
    Bs, Ts, _ = x_sample.shape
    n_pages, page = page_table.shape[1], cache_moba.shape[2]
    past_len = n_pages * page
    alpha = (2 * depth) ** 0.25
    halves = NSA_CMP_LEN // NSA_CMP_STRIDE

    w_in_p = _pack_w_in(w_in, D)
    eye_h = jnp.eye(MLA_HEADS, dtype=F32)
    w_uk_bd = jnp.einsum('lchn,hg->lhngc', mla_w_uk, eye_h).reshape(depth, MLA_HEADS * MLA_NOPE,
                                                                   MLA_HEADS * MLA_KV_LORA).astype(BF16)
    w_uv_bd = jnp.einsum('lchv,hg->lhcgv', mla_w_uv, eye_h).reshape(depth, MLA_HEADS * MLA_KV_LORA,
                                                                   MLA_HEADS * MLA_V).astype(BF16)
    w1cat = nsa_cmp_w1.reshape(depth, 2, halves, NSA_CMP_STRIDE * HEAD_DIM, NSA_CMP_HIDDEN)
    w1cat = w1cat.transpose(0, 1, 3, 2, 4).reshape(depth, 2, NSA_CMP_STRIDE * HEAD_DIM, halves * NSA_CMP_HIDDEN)
    pe = nsa_cmp_pos.reshape(depth, 2, halves, NSA_CMP_STRIDE * HEAD_DIM)
    pe = jnp.concatenate([pe, jnp.zeros((depth, 2, 8 - halves, NSA_CMP_STRIDE * HEAD_DIM), F32)], axis=2)
    w1r = nsa_cmp_w1.reshape(depth, 2, halves, NSA_CMP_STRIDE, HEAD_DIM, NSA_CMP_HIDDEN).transpose(0, 3, 1, 4, 2, 5)
    w1r = w1r.reshape(depth, NSA_CMP_STRIDE, 2, HEAD_DIM, halves * NSA_CMP_HIDDEN)
    w1bd = jnp.einsum('ljkdc,kq->ljkdqc', w1r, jnp.eye(2, dtype=F32))
    w1bd = w1bd.reshape(depth, NSA_CMP_STRIDE, 2 * HEAD_DIM, 2 * halves * NSA_CMP_HIDDEN).astype(BF16)
    router_w_pad = _pad_axis(router_w, 1, LANE)
    assert page == LANE
    n_pool = cache_moba.shape[1]
    moba_t = cache_moba.transpose(0, 1, 3, 4, 5, 2).reshape(depth, n_pool, MOBA_KV_HEADS * 2 * HEAD_DIM, page)
    nsa_t = cache_nsa.transpose(0, 1, 3, 4, 2).reshape(depth, n_pool, 4 * HEAD_DIM, page)
    mla_t = cache_mla.transpose(0, 1, 3, 2)
    win_t = state_nsa_win.transpose(0, 1, 3, 4, 2).reshape(depth, Bs, 2 * HEAD_DIM, state_nsa_win.shape[2])
    table = page_table.reshape(-1).astype(jnp.int32)
    big = dict(w_branch=w_branch.astype(BF16), w_out=w_out.astype(BF16), moe_w_gate=moe_w_gate.astype(BF16),
               moe_w_up=moe_w_up.astype(BF16), moe_w_down=moe_w_down.astype(BF16),
               ple_w_proj=ple_w_proj.astype(BF16), ple_w_gate=ple_w_gate.astype(BF16),
               mla_w_qb=mla_w_qb.astype(BF16))

    xp = x_prompt.reshape(Bp * Tp, D)
    xs = x_sample.reshape(Bs * Ts, D)
    st_p, st_s = [], []
    for i in range(depth):
        lw = {k: v[i] for k, v in big.items()}
        lw.update(w_in=w_in_p[i], mla_w_uk_bd=w_uk_bd[i], mla_w_uv_bd=w_uv_bd[i],
                  mla_q_norm=mla_q_norm[i], mla_kv_norm=mla_kv_norm[i],
                  cmp=dict(w1cat=w1cat[i], w1bd=w1bd[i], bias=_cmp_bias(pe[i], w1cat[i]), w2=nsa_cmp_w2[i]),
                  ln1_g=ln1_g[i][None], ln1_b=ln1_b[i][None], ln2_g=ln2_g[i][None], ln2_b=ln2_b[i][None])
        past = dict(moba_t=moba_t, nsa_t=nsa_t, mla_t=mla_t, win_t=win_t, win=state_nsa_win[i], table=table,
                    layer=i, len=past_len)
        xp, sp = decoder_layer(xp, p_prompt[i].reshape(Bp * Tp, -1), Bp, Tp, 0, None, lw, router_w_pad, router_bias,
                               alpha, 256)
        xs, ss = decoder_layer(xs, p_sample[i].reshape(Bs * Ts, -1), Bs, Ts, past_len, past, lw, router_w_pad,
                               router_bias, alpha, 128)
        st_p.append(sp)
        st_s.append(ss)
    outs = [xp.reshape(Bp, Tp, D), xs.reshape(Bs, Ts, D)]
    for c in range(4):
        outs.append(jnp.stack([s[c] for s in st_p]))
        outs.append(jnp.stack([s[c] for s in st_s]))
    return tuple(outs)
```

```python
import functools
import math

import numpy as np
import jax
import jax.numpy as jnp
from jax import lax
from jax.experimental import pallas as pl
from jax.experimental.pallas import tpu as pltpu

F32 = jnp.float32
BF16 = jnp.bfloat16
HIGHEST = lax.Precision.HIGHEST

HEAD_DIM = 64
MOBA_HEADS = 8
MOBA_KV_HEADS = 2
MOBA_BLOCK = 256
MOBA_TOPK = 3
NSA_HEADS = 8
NSA_CMP_LEN = 32
NSA_CMP_STRIDE = 16
NSA_CMP_HIDDEN = 128
NSA_SEL_BLOCK = 64
NSA_TOPN = 16
NSA_WINDOW = 512
MLA_HEADS = 8
MLA_Q_LORA = 384
MLA_KV_LORA = 128
MLA_NOPE = 64
MLA_ROPE = 32
MLA_V = 64
ROPE_BASE = 10000.0
N_BRANCH = 3
BRANCH_WIDTH = 512
N_EXPERTS = 16
N_GROUPS = 4
TOPK_EXPERTS = 2
LN_EPS = 1e-5
RMS_EPS = 1e-6
NEG_INF = -1e30
FORCE_BONUS = 1e4

LANE = 128
VMEM_LIMIT = 52 * 1024 * 1024

C_MQ = 0
C_MKV = 512
C_NQ = 768
C_NKV = 1280
C_NGATE = 1664
C_QA = 1792
C_KVA = 2176
C_MGATE = 2560


def _cparams(sem):
    return pltpu.CompilerParams(dimension_semantics=sem, vmem_limit_bytes=VMEM_LIMIT)


def _round_up(x, m):
    return -(-x // m) * m


def _pick_tile(n, pref):
    t = min(pref, n)
    while n % t:
        t //= 2
    return t


def _mm_kernel(x_ref, w_ref, o_ref, *, precision):
    x = x_ref[...]
    w = w_ref[...]
    if precision is None:
        x = x.astype(BF16)
        w = w.astype(BF16)
    o_ref[...] = jnp.dot(x, w, preferred_element_type=F32, precision=precision).astype(o_ref.dtype)


def mm(x, w, *, tm=512, tn=512, precision=None, out_dtype=F32, name="mm"):
    M, K = x.shape
    N = w.shape[1]
    tm = _pick_tile(M, tm)
    tn = _pick_tile(N, tn)
    return pl.pallas_call(
        functools.partial(_mm_kernel, precision=precision),
        out_shape=jax.ShapeDtypeStruct((M, N), out_dtype),
        grid=(M // tm, N // tn),
        in_specs=[pl.BlockSpec((tm, K), lambda i, j: (i, 0)),
                  pl.BlockSpec((K, tn), lambda i, j: (0, j))],
        out_specs=pl.BlockSpec((tm, tn), lambda i, j: (i, j)),
        compiler_params=_cparams(("parallel", "arbitrary")),
        name=name,
    )(x, w)


def _bmm_kernel(x_ref, w_ref, o_ref):
    o_ref[...] = jnp.dot(x_ref[...], w_ref[...], preferred_element_type=F32, precision=HIGHEST)


def bmm_f32(x, w, name="bmm"):
    B, M, K = x.shape
    N = w.shape[2]
    tm = _pick_tile(M, 1024)
    return pl.pallas_call(
        _bmm_kernel,
        out_shape=jax.ShapeDtypeStruct((B, M, N), F32),
        grid=(B, M // tm),
        in_specs=[pl.BlockSpec((None, tm, K), lambda b, i: (b, i, 0)),
                  pl.BlockSpec((None, K, N), lambda b, i: (b, 0, 0))],
        out_specs=pl.BlockSpec((None, tm, N), lambda b, i: (b, i, 0)),
        compiler_params=_cparams(("parallel", "arbitrary")),
        name=name,
    )(x, w)


def _block_mean_kernel(k_ref, o_ref, *, inv):
    o_ref[...] = jnp.sum(k_ref[...], axis=0, keepdims=True) * inv


def block_mean(k, blk):
    B, L, D = k.shape
    nb = L // blk
    out = pl.pallas_call(
        functools.partial(_block_mean_kernel, inv=1.0 / blk),
        out_shape=jax.ShapeDtypeStruct((B, nb, 1, D), F32),
        grid=(B, nb),
        in_specs=[pl.BlockSpec((None, blk, D), lambda b, n: (b, n, 0))],
        out_specs=pl.BlockSpec((None, None, 1, D), lambda b, n: (b, n, 0, 0)),
        compiler_params=_cparams(("parallel", "arbitrary")),
        name="block_mean",
    )(k)
    return out.reshape(B, nb, D)


def _flash_kernel(*refs, G, slopes, kv_heads, qpos0, kpos0, lk, window, sel_blk, scale, tr, tk, nkt):
    if sel_blk is None:
        q_ref, k_ref, v_ref, o_ref, m_sc, l_sc, acc_sc = refs
        sel_ref = None
    else:
        q_ref, k_ref, v_ref, sel_ref, o_ref, m_sc, l_sc, acc_sc = refs
    b = pl.program_id(0)
    i = pl.program_id(1)
    j = pl.program_id(2)
    shift = int(math.log2(G))

    @pl.when(j == 0)
    def _():
        m_sc[...] = jnp.full_like(m_sc, NEG_INF)
        l_sc[...] = jnp.zeros_like(l_sc)
        acc_sc[...] = jnp.zeros_like(acc_sc)

    q_lo = qpos0 + lax.shift_right_logical(i * tr, shift)
    q_hi = qpos0 + lax.shift_right_logical(i * tr + (tr - 1), shift)
    k_lo = kpos0 + j * tk
    needed = jnp.logical_and(k_lo <= q_hi, j * tk < lk)
    if window is not None:
        needed = jnp.logical_and(needed, k_lo + (tk - 1) > q_lo - window)

    @pl.when(needed)
    def _():
        q = q_ref[...].astype(BF16)
        k = k_ref[...].astype(BF16)
        s = lax.dot_general(q, k, (((1,), (1,)), ((), ())), preferred_element_type=F32) * scale
        rows = i * tr + lax.broadcasted_iota(jnp.int32, (tr, 1), 0)
        qpos = qpos0 + lax.shift_right_logical(rows, shift)
        kidx = j * tk + lax.broadcasted_iota(jnp.int32, (1, tk), 1)
        dist = qpos - (kpos0 + kidx)
        mask = jnp.logical_and(dist >= 0, kidx < lk)
        if window is not None:
            mask = jnp.logical_and(mask, dist < window)
        if slopes is not None:
            g = jnp.bitwise_and(rows, G - 1)
            head = g if kv_heads == 1 else g + G * lax.rem(b, kv_heads)
            s = s - _row_slopes(head, slopes, (tr, 1)) * dist.astype(F32)
        if sel_ref is not None:
            nbp = sel_ref.shape[-1]
            blk_of_key = lax.shift_right_logical(kidx, int(math.log2(sel_blk)))
            expand = (lax.broadcasted_iota(jnp.int32, (nbp, tk), 0) == blk_of_key).astype(BF16)
            picked = jnp.dot(sel_ref[...].astype(BF16), expand, preferred_element_type=F32)
            mask = jnp.logical_and(mask, picked > 0.5)
        s = jnp.where(mask, s, NEG_INF)
        m_prev = m_sc[...]
        m_new = jnp.maximum(m_prev, jnp.max(s, axis=-1, keepdims=True))
        alpha = jnp.exp(m_prev - m_new)
        p = jnp.where(mask, jnp.exp(s - m_new), 0.0)
        l_sc[...] = alpha * l_sc[...] + jnp.sum(p, axis=-1, keepdims=True)
        acc_sc[...] = alpha * acc_sc[...] + jnp.dot(p.astype(BF16), v_ref[...].astype(BF16),
                                                    preferred_element_type=F32)
        m_sc[...] = m_new

    @pl.when(j == nkt - 1)
    def _():
        o_ref[...] = acc_sc[...] / jnp.maximum(l_sc[...], 1e-30)


def flash(q, k, v, *, G, slopes, kv_heads=1, qpos0, kpos0=0, lk, window=None, sel=None, sel_blk=None,
          scale, tr, tk, name):
    B, R, Dk = q.shape
    Lk, Dv = v.shape[1], v.shape[2]
    tr = _pick_tile(R, tr)
    assert Lk % tk == 0 and G & (G - 1) == 0
    nkt = Lk // tk
    kern = functools.partial(_flash_kernel, G=G, slopes=slopes, kv_heads=kv_heads, qpos0=qpos0, kpos0=kpos0,
                             lk=lk, window=window, sel_blk=sel_blk, scale=scale, tr=tr, tk=tk, nkt=nkt)
    in_specs = [pl.BlockSpec((None, tr, Dk), lambda b, i, j: (b, i, 0)),
                pl.BlockSpec((None, tk, Dk), lambda b, i, j: (b, j, 0)),
                pl.BlockSpec((None, tk, Dv), lambda b, i, j: (b, j, 0))]
    args = [q, k, v]
    if sel is not None:
        in_specs.append(pl.BlockSpec((None, tr, sel.shape[-1]), lambda b, i, j: (b, i, 0)))
        args.append(sel)
    return pl.pallas_call(
        kern,
        out_shape=jax.ShapeDtypeStruct((B, R, Dv), F32),
        grid=(B, R // tr, nkt),
        in_specs=in_specs,
        out_specs=pl.BlockSpec((None, tr, Dv), lambda b, i, j: (b, i, 0)),
        scratch_shapes=[pltpu.VMEM((tr, 1), F32), pltpu.VMEM((tr, 1), F32), pltpu.VMEM((tr, Dv), F32)],
        compiler_params=_cparams(("parallel", "parallel", "arbitrary")),
        name=name,
    )(*args)


def _gelu_proj_kernel(a_ref, b_ref, bias_ref, w_ref, o_ref):
    pre = a_ref[...] + b_ref[...] + bias_ref[...]
    act = jax.nn.gelu(pre)
    o_ref[...] = jnp.dot(act.astype(BF16), w_ref[...].astype(BF16), preferred_element_type=F32)


def gelu_proj(a, b, bias, w):
    M, Fd = a.shape
    D = w.shape[1]
    tm = _pick_tile(M, 1024)
    return pl.pallas_call(
        _gelu_proj_kernel,
        out_shape=jax.ShapeDtypeStruct((M, D), F32),
        grid=(M // tm,),
        in_specs=[pl.BlockSpec((tm, Fd), lambda i: (i, 0)),
                  pl.BlockSpec((tm, Fd), lambda i: (i, 0)),
                  pl.BlockSpec((1, Fd), lambda i: (0, 0)),
                  pl.BlockSpec((Fd, D), lambda i: (0, 0))],
        out_specs=pl.BlockSpec((tm, D), lambda i: (i, 0)),
        compiler_params=_cparams(("parallel",)),
        name="gelu_proj",
    )(a, b, bias, w)


def _cmp_attn_kernel(q_ref, ck_ref, cv_ref, ovn_ref, o_ref, imp_ref, *, H, slopes, qpos0, n_cmp, scale, tr):
    i = pl.program_id(1)
    ncp = ck_ref.shape[0]
    shift = int(math.log2(H))
    q = q_ref[...].astype(BF16)
    ck = ck_ref[...].astype(BF16)
    s = lax.dot_general(q, ck, (((1,), (1,)), ((), ())), preferred_element_type=F32) * scale
    rows = i * tr + lax.broadcasted_iota(jnp.int32, (tr, 1), 0)
    qpos = qpos0 + lax.shift_right_logical(rows, shift)
    n_idx = lax.broadcasted_iota(jnp.int32, (1, ncp), 1)
    dist = qpos - (n_idx * NSA_CMP_STRIDE + (NSA_CMP_LEN - 1))
    mask = jnp.logical_and(dist >= 0, n_idx < n_cmp)
    head = jnp.bitwise_and(rows, H - 1)
    slope = jnp.zeros((tr, 1), F32)
    for h, sl in enumerate(slopes):
        slope = jnp.where(head == h, np.float32(sl), slope)
    s = s - slope * dist.astype(F32)
    s = jnp.where(mask, s, NEG_INF)
    m = jnp.max(s, axis=-1, keepdims=True)
    e = jnp.where(mask, jnp.exp(s - m), 0.0)
    p = e / jnp.maximum(jnp.sum(e, axis=-1, keepdims=True), 1e-30)
    o_ref[...] = jnp.dot(p.astype(BF16), cv_ref[...].astype(BF16), preferred_element_type=F32)
    tq = tr // H
    head_sum = (lax.shift_right_logical(lax.broadcasted_iota(jnp.int32, (tq, tr), 1), shift)
                == lax.broadcasted_iota(jnp.int32, (tq, tr), 0)).astype(F32)
    p_sum = jnp.dot(head_sum, p, preferred_element_type=F32, precision=HIGHEST)
    imp_ref[...] = jnp.dot(p_sum, ovn_ref[...], preferred_element_type=F32, precision=HIGHEST)


def cmp_attention(q, ck, cv, ovn, *, H, slopes, qpos0, n_cmp, scale, tr):
    B, R, D = q.shape
    ncp = ck.shape[1]
    nsp = ovn.shape[1]
    tr = _pick_tile(R, tr)
    return pl.pallas_call(
        functools.partial(_cmp_attn_kernel, H=H, slopes=slopes, qpos0=qpos0, n_cmp=n_cmp, scale=scale, tr=tr),
        out_shape=(jax.ShapeDtypeStruct((B, R, D), F32), jax.ShapeDtypeStruct((B, R // H, nsp), F32)),
        grid=(B, R // tr),
        in_specs=[pl.BlockSpec((None, tr, D), lambda b, i: (b, i, 0)),
                  pl.BlockSpec((None, ncp, D), lambda b, i: (b, 0, 0)),
                  pl.BlockSpec((None, ncp, D), lambda b, i: (b, 0, 0)),
                  pl.BlockSpec((ncp, nsp), lambda b, i: (0, 0))],
        out_specs=(pl.BlockSpec((None, tr, D), lambda b, i: (b, i, 0)),
                   pl.BlockSpec((None, tr // H, nsp), lambda b, i: (b, i, 0))),
        compiler_params=_cparams(("parallel", "arbitrary")),
        name="nsa_cmp_attn",
    )(q, ck, cv, ovn)


PAGES_PER_STEP = 16


def _page_specs(rows, row_block, layer, n_pages):
    def spec(p):
        return pl.BlockSpec((None, None, rows, LANE),
                            lambda b, j, tbl: (layer, tbl[b * n_pages + j * PAGES_PER_STEP + p], row_block, 0))
    return [spec(p) for p in range(PAGES_PER_STEP)]


def _row_slopes(head, slopes, shape):
    slope = jnp.zeros(shape, F32)
    for h, sl in enumerate(slopes):
        slope = jnp.where(head == h, np.float32(sl), slope)
    return slope


def _nt_dot(a, b):
    return lax.dot_general(a, b, (((1,), (1,)), ((), ())), preferred_element_type=F32)


def _moba_paged_kernel(tbl_ref, q_ref, new_ref, *rest, slopes, past_len, n_steps, T, G, scale):
    pages = rest[:PAGES_PER_STEP]
    o_ref, m_part, l_part, kmean, o_part = rest[PAGES_PER_STEP:]
    j = pl.program_id(1)
    KVH = MOBA_KV_HEADS
    R = T * G
    HD = HEAD_DIM
    blocks_per_step = PAGES_PER_STEP * LANE // MOBA_BLOCK
    pages_per_block = MOBA_BLOCK // LANE
    nb_past = past_len // MOBA_BLOCK
    lane = lax.broadcasted_iota(jnp.int32, (1, LANE), 1)
    rows = lax.broadcasted_iota(jnp.int32, (R, 1), 0)
    t_of_row = lax.shift_right_logical(rows, int(math.log2(G)))
    qpos = past_len + t_of_row

    @pl.when(j == 0)
    def _():
        m_part[...] = jnp.full_like(m_part, NEG_INF)
        l_part[...] = jnp.zeros_like(l_part)
        kmean[...] = jnp.zeros_like(kmean)

    for kvh in range(KVH):
        rs = pl.ds(kvh * R, R)
        slope = _row_slopes(jnp.bitwise_and(rows, G - 1) + kvh * G, slopes, (R, 1))
        qf = q_ref[rs, :]
        qb = qf.astype(BF16)
        m_acc = m_part[rs, :]
        l_acc = l_part[rs, :]
        km_acc = kmean[kvh]
        for c in range(blocks_per_step):
            pg = [pages[c * pages_per_block + u] for u in range(pages_per_block)]
            kt = jnp.concatenate([p_[pl.ds(kvh * 2 * HD, HD), :] for p_ in pg], axis=1)
            vt = jnp.concatenate([p_[pl.ds(kvh * 2 * HD + HD, HD), :] for p_ in pg], axis=1)
            blk = j * blocks_per_step + c
            kpos = blk * MOBA_BLOCK + lax.broadcasted_iota(jnp.int32, (1, MOBA_BLOCK), 1)
            s = jnp.dot(qb, kt.astype(BF16), preferred_element_type=F32) * scale
            s = s - slope * (qpos - kpos).astype(F32)
            m = jnp.max(s, axis=-1, keepdims=True)
            p = jnp.exp(s - m)
            here = lane == blk
            m_acc = jnp.where(here, m, m_acc)
            l_acc = jnp.where(here, jnp.sum(p, axis=-1, keepdims=True), l_acc)
            o_part[blk, rs, :] = _nt_dot(p.astype(BF16), vt.astype(BF16))
            km_acc = jnp.where(here, jnp.sum(kt, axis=-1, keepdims=True) * (1.0 / MOBA_BLOCK), km_acc)
        m_part[rs, :] = m_acc
        l_part[rs, :] = l_acc
        kmean[kvh] = km_acc

        @pl.when(j == n_steps - 1)
        def _():
            gate = jnp.dot(qf, kmean[kvh], preferred_element_type=F32, precision=HIGHEST)
            past = lane < nb_past
            gate = jnp.where(past, gate, NEG_INF)
            rank = jnp.zeros((R, LANE), jnp.int32)
            for c2 in range(nb_past):
                col = gate[:, c2:c2 + 1]
                beats = jnp.logical_or(col > gate, jnp.logical_and(col == gate, lane > c2))
                rank = rank + beats.astype(jnp.int32)
            sel = jnp.logical_and(past, rank < MOBA_TOPK)
            k_new = new_ref[:, pl.ds(kvh * 2 * HD, HD)]
            v_new = new_ref[:, pl.ds(kvh * 2 * HD + HD, HD)]
            dist = t_of_row - lax.broadcasted_iota(jnp.int32, (1, T), 1)
            ok = dist >= 0
            s_own = _nt_dot(qb, k_new.astype(BF16)) * scale - slope * dist.astype(F32)
            s_own = jnp.where(ok, s_own, NEG_INF)
            mp = m_part[rs, :]
            m_all = jnp.maximum(jnp.max(s_own, axis=-1, keepdims=True),
                                jnp.max(jnp.where(sel, mp, NEG_INF), axis=-1, keepdims=True))
            p_own = jnp.where(ok, jnp.exp(s_own - m_all), 0.0)
            w = jnp.where(sel, jnp.exp(mp - m_all), 0.0)
            l_all = jnp.sum(p_own, axis=-1, keepdims=True) + jnp.sum(w * l_part[rs, :], axis=-1, keepdims=True)
            acc = jnp.dot(p_own.astype(BF16), v_new.astype(BF16), preferred_element_type=F32)
            for c2 in range(nb_past):
                acc = acc + w[:, c2:c2 + 1] * o_part[c2, rs, :]
            o_ref[rs, :] = acc / jnp.maximum(l_all, 1e-30)


def moba_paged(q, proj, cache_t, table, layer, *, past_len, T):
    B, R2, HD = q.shape
    n_pages = past_len // LANE
    n_steps = n_pages // PAGES_PER_STEP
    G = MOBA_HEADS // MOBA_KV_HEADS
    nb_past = past_len // MOBA_BLOCK
    assert n_pages % PAGES_PER_STEP == 0 and past_len % MOBA_BLOCK == 0 and T <= MOBA_BLOCK and nb_past <= LANE
    width = MOBA_KV_HEADS * 2 * HD
    kern = functools.partial(_moba_paged_kernel, slopes=_alibi_slopes(MOBA_HEADS), past_len=past_len,
                             n_steps=n_steps, T=T, G=G, scale=HD ** -0.5)
    return pl.pallas_call(
        kern,
        out_shape=jax.ShapeDtypeStruct((B, R2, HD), F32),
        grid_spec=pltpu.PrefetchScalarGridSpec(
            num_scalar_prefetch=1, grid=(B, n_steps),
            in_specs=[pl.BlockSpec((None, R2, HD), lambda b, j, tbl: (b, 0, 0)),
                      pl.BlockSpec((T, width), lambda b, j, tbl: (b, C_MKV // width))]
            + _page_specs(width, 0, layer, n_pages),
            out_specs=pl.BlockSpec((None, R2, HD), lambda b, j, tbl: (b, 0, 0)),
            scratch_shapes=[pltpu.VMEM((R2, LANE), F32), pltpu.VMEM((R2, LANE), F32),
                            pltpu.VMEM((MOBA_KV_HEADS, HD, LANE), F32), pltpu.VMEM((nb_past, R2, HD), F32)]),
        compiler_params=_cparams(("parallel", "arbitrary")),
        name="moba_paged",
    )(table, q, proj, *([cache_t] * PAGES_PER_STEP))


def _online_update(s, mask, v_dot, m_sc, l_sc, acc_sc):
    m_prev = m_sc[...]
    m_new = jnp.maximum(m_prev, jnp.max(s, axis=-1, keepdims=True))
    alpha = jnp.exp(m_prev - m_new)
    p = jnp.exp(s - m_new)
    if mask is not None:
        p = jnp.where(mask, p, 0.0)
    l_sc[...] = alpha * l_sc[...] + jnp.sum(p, axis=-1, keepdims=True)
    acc_sc[...] = alpha * acc_sc[...] + v_dot(p.astype(BF16))
    m_sc[...] = m_new


def _flash_init(j, m_sc, l_sc, acc_sc):
    @pl.when(j == 0)
    def _():
        m_sc[...] = jnp.full_like(m_sc, NEG_INF)
        l_sc[...] = jnp.zeros_like(l_sc)
        acc_sc[...] = jnp.zeros_like(acc_sc)


def _mla_paged_kernel(tbl_ref, q_ref, new_ref, *rest, n_steps, T, H, scale):
    pages = rest[:PAGES_PER_STEP]
    o_ref, m_sc, l_sc, acc_sc = rest[PAGES_PER_STEP:]
    j = pl.program_id(1)
    C = MLA_KV_LORA
    R = T * H
    _flash_init(j, m_sc, l_sc, acc_sc)
    qb = q_ref[...].astype(BF16)
    pg = [p_[...].astype(BF16) for p_ in pages]
    s = jnp.concatenate([jnp.dot(qb, x, preferred_element_type=F32) for x in pg], axis=1) * scale

    def v_dot(p):
        out = None
        for u, x in enumerate(pg):
            term = _nt_dot(p[:, u * LANE:(u + 1) * LANE], x[:C, :])
            out = term if out is None else out + term
        return out

    _online_update(s, None, v_dot, m_sc, l_sc, acc_sc)

    @pl.when(j == n_steps - 1)
    def _():
        new = new_ref[...].astype(BF16)
        rows = lax.broadcasted_iota(jnp.int32, (R, 1), 0)
        ok = lax.shift_right_logical(rows, int(math.log2(H))) >= lax.broadcasted_iota(jnp.int32, (1, T), 1)
        s_new = jnp.where(ok, _nt_dot(qb, new) * scale, NEG_INF)
        _online_update(s_new, ok, lambda p: jnp.dot(p, new[:, :C], preferred_element_type=F32), m_sc, l_sc, acc_sc)
        o_ref[...] = acc_sc[...] / jnp.maximum(l_sc[...], 1e-30)


def mla_paged(q_abs, new_rows, cache_t, table, layer, *, past_len, T):
    B, R, Dk = q_abs.shape
    n_pages = past_len // LANE
    n_steps = n_pages // PAGES_PER_STEP
    assert n_pages % PAGES_PER_STEP == 0
    kern = functools.partial(_mla_paged_kernel, n_steps=n_steps, T=T, H=MLA_HEADS,
                             scale=(MLA_NOPE + MLA_ROPE) ** -0.5)
    return pl.pallas_call(
        kern,
        out_shape=jax.ShapeDtypeStruct((B, R, MLA_KV_LORA), F32),
        grid_spec=pltpu.PrefetchScalarGridSpec(
            num_scalar_prefetch=1, grid=(B, n_steps),
            in_specs=[pl.BlockSpec((None, R, Dk), lambda b, j, tbl: (b, 0, 0)),
                      pl.BlockSpec((None, T, Dk), lambda b, j, tbl: (b, 0, 0))]
            + _page_specs(Dk, 0, layer, n_pages),
            out_specs=pl.BlockSpec((None, R, MLA_KV_LORA), lambda b, j, tbl: (b, 0, 0)),
            scratch_shapes=[pltpu.VMEM((R, 1), F32), pltpu.VMEM((R, 1), F32), pltpu.VMEM((R, MLA_KV_LORA), F32)]),
        compiler_params=_cparams(("parallel", "arbitrary")),
        name="mla_paged",
    )(table, q_abs, new_rows, *([cache_t] * PAGES_PER_STEP))


def _nsa_cmp_paged_kernel(tbl_ref, q_ref, w1_ref, bias_ref, w2_ref, ovn_ref, *rest, slopes, past_len, n_steps, T, H,
                          n_cmp, n_sel, scale):
    pages = rest[:PAGES_PER_STEP]
    o_ref, sel_ref, xt_sc, part_sc = rest[PAGES_PER_STEP:]
    j = pl.program_id(1)
    HD, FH = HEAD_DIM, NSA_CMP_HIDDEN
    blocks_per_step = PAGES_PER_STEP * LANE // NSA_CMP_STRIDE
    m_blocks = n_steps * blocks_per_step
    R = T * H

    @pl.when(j == 0)
    def _():
        part_sc[pl.ds(m_blocks, 8), :] = jnp.zeros((8, part_sc.shape[1]), F32)

    for u, p_ in enumerate(pages):
        xt_sc[pl.ds(u * LANE, LANE), :] = p_[...].T
    acc = jnp.zeros((blocks_per_step, part_sc.shape[1]), F32)
    for jj in range(NSA_CMP_STRIDE):
        x = xt_sc[pl.ds(jj, blocks_per_step, stride=NSA_CMP_STRIDE), :]
        acc = acc + jnp.dot(x.astype(BF16), w1_ref[jj], preferred_element_type=F32)
    part_sc[pl.ds(pl.multiple_of(j * blocks_per_step, blocks_per_step), blocks_per_step), :] = acc

    @pl.when(j == n_steps - 1)
    def _():
        comp = []
        for kk in range(2):
            a0 = part_sc[pl.ds(0, m_blocks), pl.ds(kk * 2 * FH, FH)]
            a1 = part_sc[pl.ds(1, m_blocks), pl.ds(kk * 2 * FH + FH, FH)]
            act = jax.nn.gelu(a0 + a1 + bias_ref[:, pl.ds(kk * FH, FH)])
            comp.append(jnp.dot(act.astype(BF16), w2_ref[kk].astype(BF16), preferred_element_type=F32))
        qb = q_ref[...].astype(BF16)
        rows = lax.broadcasted_iota(jnp.int32, (R, 1), 0)
        shift = int(math.log2(H))
        qpos = past_len + lax.shift_right_logical(rows, shift)
        n_idx = lax.broadcasted_iota(jnp.int32, (1, m_blocks), 1)
        dist = qpos - (n_idx * NSA_CMP_STRIDE + (NSA_CMP_LEN - 1))
        mask = jnp.logical_and(dist >= 0, n_idx < n_cmp)
        slope = _row_slopes(jnp.bitwise_and(rows, H - 1), slopes, (R, 1))
        s = _nt_dot(qb, comp[0].astype(BF16)) * scale - slope * dist.astype(F32)
        s = jnp.where(mask, s, NEG_INF)
        e = jnp.where(mask, jnp.exp(s - jnp.max(s, axis=-1, keepdims=True)), 0.0)
        p = e / jnp.maximum(jnp.sum(e, axis=-1, keepdims=True), 1e-30)
        o_ref[...] = jnp.dot(p.astype(BF16), comp[1].astype(BF16), preferred_element_type=F32)
        head_sum = (lax.shift_right_logical(lax.broadcasted_iota(jnp.int32, (T, R), 1), shift)
                    == lax.broadcasted_iota(jnp.int32, (T, R), 0)).astype(F32)
        p_sum = jnp.dot(head_sum, p, preferred_element_type=F32, precision=HIGHEST)
        imp = jnp.dot(p_sum, ovn_ref[...], preferred_element_type=F32, precision=HIGHEST)
        nsp = imp.shape[1]
        lane = lax.broadcasted_iota(jnp.int32, (1, nsp), 1)
        own = lax.shift_right_logical(past_len + lax.broadcasted_iota(jnp.int32, (T, 1), 0),
                                      int(math.log2(NSA_SEL_BLOCK)))
        forced = jnp.logical_or(lane == 0, jnp.logical_or(lane == own, lane == own - 1))
        valid = jnp.logical_and(lane <= own, lane < n_sel)
        score = jnp.where(valid, imp + jnp.where(forced, FORCE_BONUS, 0.0), NEG_INF)
        rank = jnp.zeros((T, nsp), jnp.int32)
        for c2 in range(n_sel):
            col = score[:, c2:c2 + 1]
            beats = jnp.logical_or(col > score, jnp.logical_and(col == score, lane > c2))
            rank = rank + beats.astype(jnp.int32)
        sel_ref[...] = jnp.logical_and(valid, rank < NSA_TOPN).astype(F32)


def nsa_cmp_paged(q, w1bd, bias2, w2, ovn, cache_t, table, layer, *, past_len, T, n_cmp, n_sel):
    B, R, HD = q.shape
    n_pages = past_len // LANE
    n_steps = n_pages // PAGES_PER_STEP
    assert n_pages % PAGES_PER_STEP == 0
    m_blocks = past_len // NSA_CMP_STRIDE
    nsp = ovn.shape[1]
    pw = w1bd.shape[2]
    kern = functools.partial(_nsa_cmp_paged_kernel, slopes=_alibi_slopes(NSA_HEADS), past_len=past_len,
                             n_steps=n_steps, T=T, H=NSA_HEADS, n_cmp=n_cmp, n_sel=n_sel, scale=HD ** -0.5)
    return pl.pallas_call(
        kern,
        out_shape=(jax.ShapeDtypeStruct((B, R, HD), F32), jax.ShapeDtypeStruct((B, T, nsp), F32)),
        grid_spec=pltpu.PrefetchScalarGridSpec(
            num_scalar_prefetch=1, grid=(B, n_steps),
            in_specs=[pl.BlockSpec((None, R, HD), lambda b, j, tbl: (b, 0, 0)),
                      pl.BlockSpec(w1bd.shape, lambda b, j, tbl: (0, 0, 0)),
                      pl.BlockSpec(bias2.shape, lambda b, j, tbl: (0, 0)),
                      pl.BlockSpec(w2.shape, lambda b, j, tbl: (0, 0, 0)),
                      pl.BlockSpec(ovn.shape, lambda b, j, tbl: (0, 0))]
            + _page_specs(2 * HD, 0, layer, n_pages),
            out_specs=(pl.BlockSpec((None, R, HD), lambda b, j, tbl: (b, 0, 0)),
                       pl.BlockSpec((None, T, nsp), lambda b, j, tbl: (b, 0, 0))),
            scratch_shapes=[pltpu.VMEM((PAGES_PER_STEP * LANE, 2 * HD), F32), pltpu.VMEM((m_blocks + 8, pw), F32)]),
        compiler_params=_cparams(("parallel", "arbitrary")),
        name="nsa_cmp_paged",
    )(table, q, w1bd, bias2, w2, ovn, *([cache_t] * PAGES_PER_STEP))


def _nsa_sel_paged_kernel(tbl_ref, q_ref, selx_ref, selown_ref, knew_ref, wnew_ref, win_ref, *rest, slopes,
                          past_len, n_steps, T, H, scale):
    pages = rest[:PAGES_PER_STEP]
    osel_ref, owin_ref, m_sc, l_sc, acc_sc = rest[PAGES_PER_STEP:]
    j = pl.program_id(1)
    HD = HEAD_DIM
    R = T * H
    K = PAGES_PER_STEP * LANE
    blocks_per_step = K // NSA_SEL_BLOCK
    _flash_init(j, m_sc, l_sc, acc_sc)
    rows = lax.broadcasted_iota(jnp.int32, (R, 1), 0)
    t_of_row = lax.shift_right_logical(rows, int(math.log2(H)))
    qpos = past_len + t_of_row
    slope = _row_slopes(jnp.bitwise_and(rows, H - 1), slopes, (R, 1))
    qb = q_ref[...].astype(BF16)

    pg = [p_[...].astype(BF16) for p_ in pages]
    s = jnp.concatenate([jnp.dot(qb, x[:HD, :], preferred_element_type=F32) for x in pg], axis=1) * scale
    kidx = lax.broadcasted_iota(jnp.int32, (1, K), 1)
    s = s - slope * (qpos - (j * K + kidx)).astype(F32)
    expand = (lax.broadcasted_iota(jnp.int32, (blocks_per_step, K), 0)
              == lax.shift_right_logical(lax.broadcasted_iota(jnp.int32, (blocks_per_step, K), 1),
                                         int(math.log2(NSA_SEL_BLOCK)))).astype(BF16)
    picked = jnp.dot(selx_ref[...].astype(BF16), expand, preferred_element_type=F32) > 0.5
    s = jnp.where(picked, s, NEG_INF)

    def v_dot(p):
        out = None
        for u, x in enumerate(pg):
            term = _nt_dot(p[:, u * LANE:(u + 1) * LANE], x[HD:, :])
            out = term if out is None else out + term
        return out

    _online_update(s, picked, v_dot, m_sc, l_sc, acc_sc)

    dist_new = t_of_row - lax.broadcasted_iota(jnp.int32, (1, T), 1)

    @pl.when(j == 0)
    def _():
        wb = win_ref.shape[1]
        win = win_ref[...].astype(BF16)
        wnew = wnew_ref[...].astype(BF16)
        dist_w = (qpos - (past_len - wb)) - lax.broadcasted_iota(jnp.int32, (1, wb), 1)
        ok_w = jnp.logical_and(dist_w >= 0, dist_w < NSA_WINDOW)
        ok_n = jnp.logical_and(dist_new >= 0, dist_new < NSA_WINDOW)
        s_w = jnp.dot(qb, win[:HD, :], preferred_element_type=F32) * scale - slope * dist_w.astype(F32)
        s_n = _nt_dot(qb, wnew[:, :HD]) * scale - slope * dist_new.astype(F32)
        s_w = jnp.where(ok_w, s_w, NEG_INF)
        s_n = jnp.where(ok_n, s_n, NEG_INF)
        m = jnp.maximum(jnp.max(s_w, axis=-1, keepdims=True), jnp.max(s_n, axis=-1, keepdims=True))
        p_w = jnp.where(ok_w, jnp.exp(s_w - m), 0.0)
        p_n = jnp.where(ok_n, jnp.exp(s_n - m), 0.0)
        l = jnp.sum(p_w, axis=-1, keepdims=True) + jnp.sum(p_n, axis=-1, keepdims=True)
        o = _nt_dot(p_w.astype(BF16), win[HD:, :]) + jnp.dot(p_n.astype(BF16), wnew[:, HD:],
                                                             preferred_element_type=F32)
        owin_ref[...] = o / jnp.maximum(l, 1e-30)

    @pl.when(j == n_steps - 1)
    def _():
        knew = knew_ref[...].astype(BF16)
        ok = jnp.logical_and(dist_new >= 0, selown_ref[...] > 0.5)
        s_new = _nt_dot(qb, knew[:, :HD]) * scale - slope * dist_new.astype(F32)
        s_new = jnp.where(ok, s_new, NEG_INF)
        _online_update(s_new, ok, lambda p: jnp.dot(p, knew[:, HD:], preferred_element_type=F32), m_sc, l_sc, acc_sc)
        osel_ref[...] = acc_sc[...] / jnp.maximum(l_sc[...], 1e-30)


def nsa_sel_paged(q, selx, selown, proj, win_t, cache_t, table, layer, *, past_len, T):
    B, R, HD = q.shape
    n_pages = past_len // LANE
    n_steps = n_pages // PAGES_PER_STEP
    wb = win_t.shape[3]
    bps = selx.shape[3]
    kern = functools.partial(_nsa_sel_paged_kernel, slopes=_alibi_slopes(NSA_HEADS), past_len=past_len,
                             n_steps=n_steps, T=T, H=NSA_HEADS, scale=HD ** -0.5)
    out = jax.ShapeDtypeStruct((B, R, HD), F32)
    o_spec = pl.BlockSpec((None, R, HD), lambda b, j, tbl: (b, 0, 0))
    return pl.pallas_call(
        kern,
        out_shape=(out, out),
        grid_spec=pltpu.PrefetchScalarGridSpec(
            num_scalar_prefetch=1, grid=(B, n_steps),
            in_specs=[pl.BlockSpec((None, R, HD), lambda b, j, tbl: (b, 0, 0)),
                      pl.BlockSpec((None, None, R, bps), lambda b, j, tbl: (b, j, 0, 0)),
                      pl.BlockSpec((None, R, 1), lambda b, j, tbl: (b, 0, 0)),
                      pl.BlockSpec((T, 2 * HD), lambda b, j, tbl: (b, (C_NKV + 2 * HD) // (2 * HD))),
                      pl.BlockSpec((T, 2 * HD), lambda b, j, tbl: (b, (C_NKV + 4 * HD) // (2 * HD))),
                      pl.BlockSpec((None, None, 2 * HD, wb), lambda b, j, tbl: (layer, b, 0, 0))]
            + _page_specs(2 * HD, 1, layer, n_pages),
            out_specs=(o_spec, o_spec),
            scratch_shapes=[pltpu.VMEM((R, 1), F32), pltpu.VMEM((R, 1), F32), pltpu.VMEM((R, HD), F32)]),
        compiler_params=_cparams(("parallel", "arbitrary")),
        name="nsa_sel_paged",
    )(table, q, selx, selown, proj, proj, win_t, *([cache_t] * PAGES_PER_STEP))


def _merge_kernel(oa_ref, ob_ref, oc_ref, wb_ref, ga_ref, gb_ref, gc_ref, o_ref):
    acc = None
    for n, (o_r, g_r) in enumerate(((oa_ref, ga_ref), (ob_ref, gb_ref), (oc_ref, gc_ref))):
        up = jnp.dot(o_r[...].astype(BF16), wb_ref[n], preferred_element_type=F32)
        term = jax.nn.sigmoid(g_r[...]) * up
        acc = term if acc is None else acc + term
    o_ref[...] = acc.astype(o_ref.dtype)


def merge_branches(o_a, o_b, o_c, w_branch, proj, d_model, *, tm=512, tn=512):
    M = o_a.shape[0]
    tm = _pick_tile(M, tm)
    tn = _pick_tile(d_model, tn)
    nj = d_model // tn
    goff = C_MGATE // tn

    def gate_spec(n):
        return pl.BlockSpec((tm, tn), lambda i, j: (i, goff + n * nj + j))

    o_spec = pl.BlockSpec((tm, BRANCH_WIDTH), lambda i, j: (i, 0))
    return pl.pallas_call(
        _merge_kernel,
        out_shape=jax.ShapeDtypeStruct((M, d_model), BF16),
        grid=(M // tm, nj),
        in_specs=[o_spec, o_spec, o_spec,
                  pl.BlockSpec((N_BRANCH, BRANCH_WIDTH, tn), lambda i, j: (0, 0, j)),
                  gate_spec(0), gate_spec(1), gate_spec(2)],
        out_specs=pl.BlockSpec((tm, tn), lambda i, j: (i, j)),
        compiler_params=_cparams(("parallel", "arbitrary")),
        name="merge_branches",
    )(o_a, o_b, o_c, w_branch, proj, proj, proj)


def _layer_norm_rows(v, g, b):
    mu = jnp.mean(v, axis=-1, keepdims=True)
    var = jnp.mean(jnp.square(v - mu), axis=-1, keepdims=True)
    return (v - mu) * lax.rsqrt(var + LN_EPS) * g + b


def _out_ln_kernel(m_ref, w_ref, x_ref, g_ref, b_ref, o_ref, *, alpha):
    mix = jnp.dot(m_ref[...].astype(BF16), w_ref[...], preferred_element_type=F32)
    o_ref[...] = _layer_norm_rows(alpha * x_ref[...] + mix, g_ref[...], b_ref[...])


def out_proj_ln(merged, w_out, x, g, b, *, alpha, tm=256):
    M, D = x.shape
    tm = _pick_tile(M, tm)
    return pl.pallas_call(
        functools.partial(_out_ln_kernel, alpha=alpha),
        out_shape=jax.ShapeDtypeStruct((M, D), F32),
        grid=(M // tm,),
        in_specs=[pl.BlockSpec((tm, D), lambda i: (i, 0)),
                  pl.BlockSpec((D, D), lambda i: (0, 0)),
                  pl.BlockSpec((tm, D), lambda i: (i, 0)),
                  pl.BlockSpec((1, D), lambda i: (0, 0)),
                  pl.BlockSpec((1, D), lambda i: (0, 0))],
        out_specs=pl.BlockSpec((tm, D), lambda i: (i, 0)),
        compiler_params=_cparams(("parallel",)),
        name="out_proj_ln1",
    )(merged, w_out, x, g, b)


def _moe_up_kernel(te_ref, nt_ref, xs_ref, wg_ref, wu_ref, rw_ref, o_ref):
    i = pl.program_id(0)

    @pl.when(i < nt_ref[0])
    def _():
        x = xs_ref[...]
        gate = jnp.dot(x, wg_ref[...], preferred_element_type=F32)
        up = jnp.dot(x, wu_ref[...], preferred_element_type=F32)
        o_ref[...] = (jax.nn.silu(gate) * up * rw_ref[...]).astype(o_ref.dtype)

    @pl.when(i >= nt_ref[0])
    def _():
        o_ref[...] = jnp.zeros_like(o_ref)


def _moe_down_kernel(te_ref, nt_ref, h_ref, wd_ref, o_ref):
    i = pl.program_id(0)

    @pl.when(i < nt_ref[0])
    def _():
        o_ref[...] = jnp.dot(h_ref[...], wd_ref[...], preferred_element_type=F32)

    @pl.when(i >= nt_ref[0])
    def _():
        o_ref[...] = jnp.zeros_like(o_ref)


def moe_experts(xs, row_w, tile_expert, n_tiles, w_gate, w_up, w_down, *, tm):
    P, D = xs.shape
    Fe = w_gate.shape[2]
    nt = P // tm
    hidden = pl.pallas_call(
        _moe_up_kernel,
        out_shape=jax.ShapeDtypeStruct((P, Fe), BF16),
        grid_spec=pltpu.PrefetchScalarGridSpec(
            num_scalar_prefetch=2, grid=(nt,),
            in_specs=[pl.BlockSpec((tm, D), lambda i, te, n: (i, 0)),
                      pl.BlockSpec((None, D, Fe), lambda i, te, n: (te[i], 0, 0)),
                      pl.BlockSpec((None, D, Fe), lambda i, te, n: (te[i], 0, 0)),
                      pl.BlockSpec((tm, 1), lambda i, te, n: (i, 0))],
            out_specs=pl.BlockSpec((tm, Fe), lambda i, te, n: (i, 0))),
        compiler_params=_cparams(("arbitrary",)),
        name="moe_up",
    )(tile_expert, n_tiles, xs, w_gate, w_up, row_w)
    return pl.pallas_call(
        _moe_down_kernel,
        out_shape=jax.ShapeDtypeStruct((P, D), F32),
        grid_spec=pltpu.PrefetchScalarGridSpec(
            num_scalar_prefetch=2, grid=(nt,),
            in_specs=[pl.BlockSpec((tm, Fe), lambda i, te, n: (i, 0)),
                      pl.BlockSpec((None, Fe, D), lambda i, te, n: (te[i], 0, 0))],
            out_specs=pl.BlockSpec((tm, D), lambda i, te, n: (i, 0))),
        compiler_params=_cparams(("arbitrary",)),
        name="moe_down",
    )(tile_expert, n_tiles, hidden, w_down)


def _ffn_ln_kernel(h_ref, f_ref, p_ref, wg_ref, wp_ref, g_ref, b_ref, o_ref, *, alpha):
    h = h_ref[...]
    gate = jax.nn.sigmoid(jnp.dot(h.astype(BF16), wg_ref[...], preferred_element_type=F32))
    ple = gate * jnp.dot(p_ref[...].astype(BF16), wp_ref[...], preferred_element_type=F32)
    o_ref[...] = _layer_norm_rows(alpha * h + f_ref[...] + ple, g_ref[...], b_ref[...])


def ffn_ple_ln(h, ffn, p, w_pg, w_pp, g, b, *, alpha, tm=256):
    M, D = h.shape
    Pd = p.shape[1]
    tm = _pick_tile(M, tm)
    return pl.pallas_call(
        functools.partial(_ffn_ln_kernel, alpha=alpha),
        out_shape=jax.ShapeDtypeStruct((M, D), F32),
        grid=(M // tm,),
        in_specs=[pl.BlockSpec((tm, D), lambda i: (i, 0)),
                  pl.BlockSpec((tm, D), lambda i: (i, 0)),
                  pl.BlockSpec((tm, Pd), lambda i: (i, 0)),
                  pl.BlockSpec((D, D), lambda i: (0, 0)),
                  pl.BlockSpec((Pd, D), lambda i: (0, 0)),
                  pl.BlockSpec((1, D), lambda i: (0, 0)),
                  pl.BlockSpec((1, D), lambda i: (0, 0))],
        out_specs=pl.BlockSpec((tm, D), lambda i: (i, 0)),
        compiler_params=_cparams(("parallel",)),
        name="ffn_ple_ln2",
    )(h, ffn, p, w_pg, w_pp, g, b)


def _alibi_slopes(n):
    return tuple(np.asarray(2.0 ** (-8.0 * np.arange(1, n + 1) / n), dtype=np.float32).tolist())


def _rms_norm(x, g):
    return x * lax.rsqrt(jnp.square(x).mean(-1, keepdims=True) + RMS_EPS) * g


def _rope(x, pos):
    half = x.shape[-1] // 2
    freqs = ROPE_BASE ** (-jnp.arange(half, dtype=F32) / half)
    ang = pos.astype(F32)[:, None] * freqs[None, :]
    shape = (pos.shape[0],) + (1,) * (x.ndim - 3) + (half,)
    cos = jnp.cos(ang).reshape(shape)
    sin = jnp.sin(ang).reshape(shape)
    x1, x2 = x[..., :half], x[..., half:]
    return jnp.concatenate([x1 * cos - x2 * sin, x1 * sin + x2 * cos], -1)


def _pad_axis(x, axis, size):
    if x.shape[axis] == size:
        return x
    pad = [(0, 0)] * x.ndim
    pad[axis] = (0, size - x.shape[axis])
    return jnp.pad(x, pad)


def _topk_mask(score, k):
    n = score.shape[-1]
    mine = score[..., :, None]
    other = score[..., None, :]
    idx = jnp.arange(n)
    beats = (other > mine) | ((other == mine) & (idx[None, :] < idx[:, None]))
    return beats.sum(-1) < k


def _overlap_weights(m, n_cmp, n_sel, nsp):
    cst = np.arange(m) * NSA_CMP_STRIDE
    sst = np.arange(n_sel) * NSA_SEL_BLOCK
    ov = np.clip(np.minimum(cst[:, None] + NSA_CMP_LEN, sst[None, :] + NSA_SEL_BLOCK)
                 - np.maximum(cst[:, None], sst[None, :]), 0, None)
    ovn = np.zeros((m, nsp), np.float32)
    ovn[:n_cmp, :n_sel] = ov[:n_cmp].astype(np.float32) / NSA_CMP_LEN
    return jnp.asarray(ovn)


def _cmp_bias(pe, w1cat):
    out = []
    for kk in range(2):
        pe_part = mm(pe[kk], w1cat[kk], tn=2 * NSA_CMP_HIDDEN, name="nsa_cmp_pos")
        out.append(pe_part[0, :NSA_CMP_HIDDEN] + pe_part[1, NSA_CMP_HIDDEN:])
    return jnp.stack(out)


def moba_attention(mq, k_full, v_full, q_pos0):
    B, T, _ = mq.shape
    H, KVH, HD = MOBA_HEADS, MOBA_KV_HEADS, HEAD_DIM
    G = H // KVH
    L = k_full.shape[1]
    nb = -(-L // MOBA_BLOCK)
    Lp = nb * MOBA_BLOCK
    kf = _pad_axis(k_full, 1, Lp).transpose(0, 2, 1, 3).reshape(B * KVH, Lp, HD)
    vf = _pad_axis(v_full, 1, Lp).transpose(0, 2, 1, 3).reshape(B * KVH, Lp, HD)
    q = mq.reshape(B, T, KVH, G, HD).transpose(0, 2, 1, 3, 4).reshape(B * KVH, T * G, HD)
    k_mean = block_mean(kf, MOBA_BLOCK)
    nbp = _round_up(nb, LANE)
    gate = bmm_f32(q, _pad_axis(k_mean.transpose(0, 2, 1), 2, nbp), name="moba_gate")[..., :nb]
    q_pos = q_pos0 + jnp.arange(T, dtype=jnp.int32)
    own = jnp.repeat(q_pos // MOBA_BLOCK, G)
    past_blk = jnp.arange(nb)[None, :] < own[:, None]
    gate = jnp.where(past_blk[None], gate, NEG_INF)
    picked = (_topk_mask(gate, min(MOBA_TOPK, nb)) & past_blk[None]) | (jnp.arange(nb)[None, :] == own[:, None])[None]
    sel = _pad_axis(picked.astype(F32), 2, nbp)
    out = flash(q, kf, vf, G=G, slopes=_alibi_slopes(H), kv_heads=KVH, qpos0=q_pos0, lk=L, sel=sel,
                sel_blk=MOBA_BLOCK, scale=HD ** -0.5, tr=1024, tk=MOBA_BLOCK, name="moba_attn")
    return out.reshape(B, KVH, T, G, HD).transpose(0, 2, 1, 3, 4).reshape(B, T, H * HD)


def nsa_attention(nq, ngate, cmp_k, cmp_v, sel_k, sel_v, win_k, win_v, win_pos0, q_pos0, T, cw):
    B = nq.shape[0]
    H, HD = NSA_HEADS, HEAD_DIM
    slopes = _alibi_slopes(H)
    scale = HD ** -0.5
    L = cmp_k.shape[1]
    q = nq.reshape(B, T * H, HD)
    m = L // NSA_CMP_STRIDE
    n_cmp = m - NSA_CMP_LEN // NSA_CMP_STRIDE + 1
    comp = []
    for kk, rows in enumerate((cmp_k, cmp_v)):
        sub = rows[:, :m * NSA_CMP_STRIDE].reshape(B * m, NSA_CMP_STRIDE * HD)
        part = mm(sub, cw['w1cat'][kk], tm=1024, tn=2 * NSA_CMP_HIDDEN, name="nsa_cmp_w1")
        part = part.reshape(B, m, 2 * NSA_CMP_HIDDEN)
        a0 = part[:, :, :NSA_CMP_HIDDEN]
        a1 = jnp.concatenate([part[:, 1:, NSA_CMP_HIDDEN:], jnp.zeros((B, 1, NSA_CMP_HIDDEN), F32)], axis=1)
        c = gelu_proj(a0.reshape(B * m, -1), a1.reshape(B * m, -1), cw['bias'][kk][None, :], cw['w2'][kk])
        comp.append(c.reshape(B, m, HD))
    n_sel = -(-L // NSA_SEL_BLOCK)
    nsp = _round_up(n_sel, LANE)
    o_cmp, imp = cmp_attention(q, comp[0], comp[1], _overlap_weights(m, n_cmp, n_sel, nsp), H=H, slopes=slopes,
                               qpos0=q_pos0, n_cmp=n_cmp, scale=scale, tr=512)
    imp = imp[..., :n_sel]
    q_pos = q_pos0 + jnp.arange(T, dtype=jnp.int32)
    own = q_pos // NSA_SEL_BLOCK
    jj = jnp.arange(n_sel)[None, :]
    forced = (jj == 0) | (jj == own[:, None]) | (jj == own[:, None] - 1)
    valid = jj <= own[:, None]
    score = jnp.where(valid[None], imp + jnp.where(forced[None], FORCE_BONUS, 0.0), NEG_INF)
    sel = _topk_mask(score, min(NSA_TOPN, n_sel)) & valid[None]
    sel = _pad_axis(sel.astype(F32), 2, nsp)
    sel = jnp.repeat(sel, H, axis=1)
    tk = 512
    Lp = _round_up(L, tk)
    o_sel = flash(q, _pad_axis(sel_k, 1, Lp), _pad_axis(sel_v, 1, Lp), G=H, slopes=slopes, qpos0=q_pos0, lk=L,
                  sel=sel, sel_blk=NSA_SEL_BLOCK, scale=scale, tr=1024, tk=tk, name="nsa_sel_attn")
    Lw = win_k.shape[1]
    tkw = 512 if Lw >= 512 else _round_up(Lw, LANE)
    Lwp = _round_up(Lw, tkw)
    o_win = flash(q, _pad_axis(win_k, 1, Lwp), _pad_axis(win_v, 1, Lwp), G=H, slopes=slopes, qpos0=q_pos0,
                  kpos0=win_pos0, lk=Lw, window=NSA_WINDOW, scale=scale, tr=1024, tk=tkw, name="nsa_win_attn")
    g = jax.nn.sigmoid(ngate.reshape(B, T, 3, H))[..., None]
    o = (g[:, :, 0] * o_cmp.reshape(B, T, H, HD) + g[:, :, 1] * o_sel.reshape(B, T, H, HD)
         + g[:, :, 2] * o_win.reshape(B, T, H, HD))
    return o.reshape(B, T, H * HD)


def moba_attention_paged(mq, proj, past, T):
    B = mq.shape[0]
    H, KVH, HD = MOBA_HEADS, MOBA_KV_HEADS, HEAD_DIM
    G = H // KVH
    q = mq.reshape(B, T, KVH, G, HD).transpose(0, 2, 1, 3, 4).reshape(B, KVH * T * G, HD)
    out = moba_paged(q, proj, past['moba_t'], past['table'], past['layer'], past_len=past['len'], T=T)
    return out.reshape(B, KVH, T, G, HD).transpose(0, 2, 1, 3, 4).reshape(B, T, H * HD)


def nsa_attention_paged(nq, ngate, proj, past, T, cw):
    B = nq.shape[0]
    H, HD = NSA_HEADS, HEAD_DIM
    past_len = past['len']
    L = past_len + T
    m = L // NSA_CMP_STRIDE
    assert m == past_len // NSA_CMP_STRIDE and past_len % NSA_SEL_BLOCK == 0 and T <= NSA_SEL_BLOCK
    n_cmp = m - NSA_CMP_LEN // NSA_CMP_STRIDE + 1
    n_sel = -(-L // NSA_SEL_BLOCK)
    nsp = _round_up(n_sel, LANE)
    q = nq.reshape(B, T * H, HD)
    o_cmp, sel = nsa_cmp_paged(q, cw['w1bd'], cw['bias'].reshape(1, -1), cw['w2'], _overlap_weights(m, n_cmp, n_sel, nsp),
                               past['nsa_t'], past['table'], past['layer'], past_len=past_len, T=T, n_cmp=n_cmp,
                               n_sel=n_sel)
    n_past_blk = past_len // NSA_SEL_BLOCK
    bps = PAGES_PER_STEP * LANE // NSA_SEL_BLOCK
    selx = sel[:, :, :n_past_blk].reshape(B, T, n_past_blk // bps, bps).transpose(0, 2, 1, 3)
    selx = jnp.repeat(selx, H, axis=2)
    selown = jnp.repeat(sel[:, :, n_past_blk], H, axis=1)[..., None]
    o_sel, o_win = nsa_sel_paged(q, selx, selown, proj, past['win_t'], past['nsa_t'], past['table'], past['layer'],
                                 past_len=past_len, T=T)
    g = jax.nn.sigmoid(ngate.reshape(B, T, 3, H))[..., None]
    o = (g[:, :, 0] * o_cmp.reshape(B, T, H, HD) + g[:, :, 1] * o_sel.reshape(B, T, H, HD)
         + g[:, :, 2] * o_win.reshape(B, T, H, HD))
    return o.reshape(B, T, H * HD)


def mla_attention(q_abs, lat_full, q_pos0, T):
    L = lat_full.shape[1]
    tk = 512
    Lp = _round_up(L, tk)
    kf = _pad_axis(lat_full, 1, Lp)
    return flash(q_abs, kf, kf[..., :MLA_KV_LORA], G=MLA_HEADS, slopes=None, qpos0=q_pos0, lk=L,
                 scale=(MLA_NOPE + MLA_ROPE) ** -0.5, tr=1024, tk=tk, name="mla_attn")


def token_mixer(x2, B, T, q_pos0, past, lw):
    N = B * T
    pos = q_pos0 + jnp.arange(T, dtype=jnp.int32)
    proj = mm(x2, lw['w_in'], tm=1024, tn=512, name="in_proj")
    mq = proj[:, C_MQ:C_MQ + 512].reshape(B, T, 512)
    moba_rows = proj[:, C_MKV:C_MKV + 256].reshape(B, T, MOBA_KV_HEADS, 2, HEAD_DIM)
    nq = proj[:, C_NQ:C_NQ + 512].reshape(B, T, 512)
    nsa_kv = proj[:, C_NKV:C_NKV + 384].reshape(B, T, 6, HEAD_DIM)
    ngate = proj[:, C_NGATE:C_NGATE + 3 * NSA_HEADS]
    qa = proj[:, C_QA:C_QA + MLA_Q_LORA]
    kva = proj[:, C_KVA:C_KVA + MLA_KV_LORA + MLA_ROPE].reshape(B, T, -1)
    nsa_rows, win_rows = nsa_kv[:, :, :4], nsa_kv[:, :, 4:]
    mla_rows = jnp.concatenate([_rms_norm(kva[..., :MLA_KV_LORA], lw['mla_kv_norm']),
                                _rope(kva[..., MLA_KV_LORA:], pos)], -1)
    q_mla = mm(_rms_norm(qa, lw['mla_q_norm']), lw['mla_w_qb'], name="mla_q_up")
    q_mla = q_mla.reshape(B, T, MLA_HEADS, MLA_NOPE + MLA_ROPE)
    q_lat = mm(q_mla[..., :MLA_NOPE].reshape(N, -1), lw['mla_w_uk_bd'], name="mla_q_absorb")
    q_rope = _rope(q_mla[..., MLA_NOPE:], pos)
    q_abs = jnp.concatenate([q_lat.reshape(B, T, MLA_HEADS, MLA_KV_LORA), q_rope], -1)
    q_abs = q_abs.reshape(B, T * MLA_HEADS, MLA_KV_LORA + MLA_ROPE)

    if past is None:
        new_win = win_rows[:, -min(NSA_WINDOW, T):]
        o_a = moba_attention(mq, moba_rows[:, :, :, 0], moba_rows[:, :, :, 1], q_pos0)
        o_b = nsa_attention(nq, ngate, nsa_rows[:, :, 0], nsa_rows[:, :, 1], nsa_rows[:, :, 2], nsa_rows[:, :, 3],
                            win_rows[:, :, 0], win_rows[:, :, 1], 0, q_pos0, T, lw['cmp'])
        o_lat = mla_attention(q_abs, mla_rows, q_pos0, T)
    else:
        new_win = jnp.concatenate([past['win'][:, T:], win_rows], axis=1)
        o_a = moba_attention_paged(mq, proj, past, T)
        o_b = nsa_attention_paged(nq, ngate, proj, past, T, lw['cmp'])
        o_lat = mla_paged(q_abs, mla_rows, past['mla_t'], past['table'], past['layer'], past_len=q_pos0, T=T)
    o_c = mm(o_lat.reshape(N, MLA_HEADS * MLA_KV_LORA), lw['mla_w_uv_bd'], name="mla_v_up")
    d_model = x2.shape[1]
    merged = merge_branches(o_a.reshape(N, -1), o_b.reshape(N, -1), o_c, lw['w_branch'], proj, d_model)
    return merged, moba_rows, nsa_rows, mla_rows, new_win


def moe_ffn(h, router_w_pad, router_bias, lw, *, tm):
    n, D = h.shape
    per_group = N_EXPERTS // N_GROUPS
    logits = mm(h, router_w_pad, precision=HIGHEST, name="router")[:, :N_EXPERTS]
    aff = jax.nn.sigmoid(logits)
    biased = aff + router_bias
    grouped = biased.reshape(n, N_GROUPS, per_group)
    group_score = jnp.where(_topk_mask(grouped, TOPK_EXPERTS), grouped, 0.0).sum(-1)
    in_group = jnp.repeat(_topk_mask(group_score, 1), per_group, axis=1)
    picked = _topk_mask(jnp.where(in_group, biased, NEG_INF), TOPK_EXPERTS)
    nth = jnp.cumsum(picked.astype(jnp.int32), axis=1)
    eidx = jnp.arange(N_EXPERTS, dtype=jnp.int32)[None, :]
    slots = [picked & (nth == k + 1) for k in range(TOPK_EXPERTS)]
    expert_idx = jnp.stack([jnp.where(s_, eidx, 0).sum(-1) for s_ in slots], axis=-1)
    w = jnp.stack([jnp.where(s_, aff, 0.0).sum(-1) for s_ in slots], axis=-1)
    w = w / w.sum(-1, keepdims=True)
    A = n * TOPK_EXPERTS
    e_flat = expert_idx.reshape(A).astype(jnp.int32)
    w_flat = w.reshape(A)
    onehot = (e_flat[:, None] == jnp.arange(N_EXPERTS, dtype=jnp.int32)[None, :]).astype(jnp.int32)
    running = jnp.cumsum(onehot, axis=0)
    rank = jnp.take_along_axis(running, e_flat[:, None], axis=1)[:, 0] - 1
    counts = running[-1]
    padded = -(-counts // tm) * tm
    pend = jnp.cumsum(padded)
    pstart = pend - padded
    dest = pstart[e_flat] + rank
    P = _round_up(A, tm) + N_EXPERTS * tm
    slot_token = jnp.zeros((P,), jnp.int32).at[dest].set(jnp.arange(A, dtype=jnp.int32) // TOPK_EXPERTS)
    slot_w = jnp.zeros((P,), F32).at[dest].set(w_flat)
    pos = dest.reshape(n, TOPK_EXPERTS)
    tile_start = jnp.arange(P // tm, dtype=jnp.int32) * tm
    tile_expert = jnp.minimum((pend[None, :] <= tile_start[:, None]).sum(-1), N_EXPERTS - 1).astype(jnp.int32)
    n_tiles = (pend[-1] // tm).astype(jnp.int32).reshape(1)
    xs = h.astype(BF16)[slot_token]
    y = moe_experts(xs, slot_w[:, None], tile_expert, n_tiles, lw['moe_w_gate'], lw['moe_w_up'], lw['moe_w_down'],
                    tm=tm)
    return y[pos[:, 0]] + y[pos[:, 1]]


def decoder_layer(x2, p2, B, T, q_pos0, past, lw, router_w_pad, router_bias, alpha, moe_tm):
    merged, moba_r, nsa_r, mla_r, win_s = token_mixer(x2, B, T, q_pos0, past, lw)
    h = out_proj_ln(merged, lw['w_out'], x2, lw['ln1_g'], lw['ln1_b'], alpha=alpha)
    ffn = moe_ffn(h, router_w_pad, router_bias, lw, tm=moe_tm)
    y = ffn_ple_ln(h, ffn, p2, lw['ple_w_gate'], lw['ple_w_proj'], lw['ln2_g'], lw['ln2_b'], alpha=alpha)
    return y, (moba_r, nsa_r, mla_r, win_s)


def _pack_w_in(w_in, d_model):
    sizes = (512, 256, 512, 384, 3 * NSA_HEADS, MLA_Q_LORA, MLA_KV_LORA + MLA_ROPE, N_BRANCH * d_model)
    starts = (C_MQ, C_MKV, C_NQ, C_NKV, C_NGATE, C_QA, C_KVA, C_MGATE)
    total = C_MGATE + N_BRANCH * d_model
    total = _round_up(total, 512)
    out = jnp.zeros(w_in.shape[:2] + (total,), BF16)
    off = 0
    for s, c in zip(sizes, starts):
        out = lax.dynamic_update_slice_in_dim(out, w_in[..., off:off + s].astype(BF16), c, axis=2)
        off += s
    return out


def kernel(x_prompt, x_sample, cache_moba, cache_nsa, cache_mla, state_nsa_win, page_table, p_prompt, p_sample,
           w_in, mla_q_norm, mla_w_qb, mla_kv_norm, mla_w_uk, mla_w_uv, nsa_cmp_pos, nsa_cmp_w1, nsa_cmp_w2,
           w_branch, w_out, ln1_g, ln1_b, router_w, router_bias, moe_w_gate, moe_w_up, moe_w_down,
           ple_w_proj, ple_w_gate, ln2_g, ln2_b):
    depth = w_in.shape[0]
    Bp, Tp, D = x_prompt.shape
    Bs, Ts, _ = x_sample.shape
    n_pages, page = page_table.shape[1], cache_moba.shape[2]
    past_len = n_pages * page
    alpha = (2 * depth) ** 0.25
    halves = NSA_CMP_LEN // NSA_CMP_STRIDE

    w_in_p = _pack_w_in(w_in, D)
    eye_h = jnp.eye(MLA_HEADS, dtype=F32)
    w_uk_bd = jnp.einsum('lchn,hg->lhngc', mla_w_uk, eye_h).reshape(depth, MLA_HEADS * MLA_NOPE,
                                                                   MLA_HEADS * MLA_KV_LORA).astype(BF16)
    w_uv_bd = jnp.einsum('lchv,hg->lhcgv', mla_w_uv, eye_h).reshape(depth, MLA_HEADS * MLA_KV_LORA,
                                                                   MLA_HEADS * MLA_V).astype(BF16)
    w1cat = nsa_cmp_w1.reshape(depth, 2, halves, NSA_CMP_STRIDE * HEAD_DIM, NSA_CMP_HIDDEN)
    w1cat = w1cat.transpose(0, 1, 3, 2, 4).reshape(depth, 2, NSA_CMP_STRIDE * HEAD_DIM, halves * NSA_CMP_HIDDEN)
    pe = nsa_cmp_pos.reshape(depth, 2, halves, NSA_CMP_STRIDE * HEAD_DIM)
    pe = jnp.concatenate([pe, jnp.zeros((depth, 2, 8 - halves, NSA_CMP_STRIDE * HEAD_DIM), F32)], axis=2)
    w1r = nsa_cmp_w1.reshape(depth, 2, halves, NSA_CMP_STRIDE, HEAD_DIM, NSA_CMP_HIDDEN).transpose(0, 3, 1, 4, 2, 5)
    w1r = w1r.reshape(depth, NSA_CMP_STRIDE, 2, HEAD_DIM, halves * NSA_CMP_HIDDEN)
    w1bd = jnp.einsum('ljkdc,kq->ljkdqc', w1r, jnp.eye(2, dtype=F32))
    w1bd = w1bd.reshape(depth, NSA_CMP_STRIDE, 2 * HEAD_DIM, 2 * halves * NSA_CMP_HIDDEN).astype(BF16)
    router_w_pad = _pad_axis(router_w, 1, LANE)
    assert page == LANE
    n_pool = cache_moba.shape[1]
    moba_t = cache_moba.transpose(0, 1, 3, 4, 5, 2).reshape(depth, n_pool, MOBA_KV_HEADS * 2 * HEAD_DIM, page)
    nsa_t = cache_nsa.transpose(0, 1, 3, 4, 2).reshape(depth, n_pool, 4 * HEAD_DIM, page)
    mla_t = cache_mla.transpose(0, 1, 3, 2)
    win_t = state_nsa_win.transpose(0, 1, 3, 4, 2).reshape(depth, Bs, 2 * HEAD_DIM, state_nsa_win.shape[2])
    table = page_table.reshape(-1).astype(jnp.int32)
    big = dict(w_branch=w_branch.astype(BF16), w_out=w_out.astype(BF16), moe_w_gate=moe_w_gate.astype(BF16),
               moe_w_up=moe_w_up.astype(BF16), moe_w_down=moe_w_down.astype(BF16),
               ple_w_proj=ple_w_proj.astype(BF16), ple_w_gate=ple_w_gate.astype(BF16),
               mla_w_qb=mla_w_qb.astype(BF16))

    xp = x_prompt.reshape(Bp * Tp, D)
    xs = x_sample.reshape(Bs * Ts, D)
    st_p, st_s = [], []
    for i in range(depth):
        lw = {k: v[i] for k, v in big.items()}
        lw.update(w_in=w_in_p[i], mla_w_uk_bd=w_uk_bd[i], mla_w_uv_bd=w_uv_bd[i],
                  mla_q_norm=mla_q_norm[i], mla_kv_norm=mla_kv_norm[i],
                  cmp=dict(w1cat=w1cat[i], w1bd=w1bd[i], bias=_cmp_bias(pe[i], w1cat[i]), w2=nsa_cmp_w2[i]),
                  ln1_g=ln1_g[i][None], ln1_b=ln1_b[i][None], ln2_g=ln2_g[i][None], ln2_b=ln2_b[i][None])
        past = dict(moba_t=moba_t, nsa_t=nsa_t, mla_t=mla_t, win_t=win_t, win=state_nsa_win[i], table=table,
                    layer=i, len=past_len)
        xp, sp = decoder_layer(xp, p_prompt[i].reshape(Bp * Tp, -1), Bp, Tp, 0, None, lw, router_w_pad, router_bias,
                               alpha, 256)
        xs, ss = decoder_layer(xs, p_sample[i].reshape(Bs * Ts, -1), Bs, Ts, past_len, past, lw, router_w_pad,
                               router_bias, alpha, 128)
        st_p.append(sp)
        st_s.append(ss)
    outs = [xp.reshape(Bp, Tp, D), xs.reshape(Bs, Ts, D)]
    for c in range(4):
        outs.append(jnp.stack([s[c] for s in st_p]))
        outs.append(jnp.stack([s[c] for s in st_s]))
    return tuple(outs)
```

```python
import functools
import math

import numpy as np
import jax
import jax.numpy as jnp
from jax import lax
from jax.experimental import pallas as pl
from jax.experimental.pallas import tpu as pltpu

F32 = jnp.float32
BF16 = jnp.bfloat16
HIGHEST = lax.Precision.HIGHEST

HEAD_DIM = 64
MOBA_HEADS = 8
MOBA_KV_HEADS = 2
MOBA_BLOCK = 256
MOBA_TOPK = 3
NSA_HEADS = 8
NSA_CMP_LEN = 32
NSA_CMP_STRIDE = 16
NSA_CMP_HIDDEN = 128
NSA_SEL_BLOCK = 64
NSA_TOPN = 16
NSA_WINDOW = 512
MLA_HEADS = 8
MLA_Q_LORA = 384
MLA_KV_LORA = 128
MLA_NOPE = 64
MLA_ROPE = 32
MLA_V = 64
ROPE_BASE = 10000.0
N_BRANCH = 3
BRANCH_WIDTH = 512
N_EXPERTS = 16
N_GROUPS = 4
TOPK_EXPERTS = 2
LN_EPS = 1e-5
RMS_EPS = 1e-6
NEG_INF = -1e30
FORCE_BONUS = 1e4

LANE = 128
VMEM_LIMIT = 52 * 1024 * 1024

C_MQ = 0
C_MKV = 512
C_NQ = 768
C_NKV = 1280
C_NGATE = 1664
C_QA = 1792
C_KVA = 2176
C_MGATE = 2560


def _cparams(sem):
    return pltpu.CompilerParams(dimension_semantics=sem, vmem_limit_bytes=VMEM_LIMIT)


def _round_up(x, m):
    return -(-x // m) * m


def _pick_tile(n, pref):
    t = min(pref, n)
    while n % t:
        t //= 2
    return t


def _mm_kernel(x_ref, w_ref, o_ref, *, precision):
    x = x_ref[...]
    w = w_ref[...]
    if precision is None:
        x = x.astype(BF16)
        w = w.astype(BF16)
    o_ref[...] = jnp.dot(x, w, preferred_element_type=F32, precision=precision).astype(o_ref.dtype)


def mm(x, w, *, tm=512, tn=512, precision=None, out_dtype=F32, name="mm"):
    M, K = x.shape
    N = w.shape[1]
    tm = _pick_tile(M, tm)
    tn = _pick_tile(N, tn)
    return pl.pallas_call(
        functools.partial(_mm_kernel, precision=precision),
        out_shape=jax.ShapeDtypeStruct((M, N), out_dtype),
        grid=(M // tm, N // tn),
        in_specs=[pl.BlockSpec((tm, K), lambda i, j: (i, 0)),
                  pl.BlockSpec((K, tn), lambda i, j: (0, j))],
        out_specs=pl.BlockSpec((tm, tn), lambda i, j: (i, j)),
        compiler_params=_cparams(("parallel", "arbitrary")),
        name=name,
    )(x, w)


def _bmm_kernel(x_ref, w_ref, o_ref):
    o_ref[...] = jnp.dot(x_ref[...], w_ref[...], preferred_element_type=F32, precision=HIGHEST)


def bmm_f32(x, w, name="bmm"):
    B, M, K = x.shape
    N = w.shape[2]
    tm = _pick_tile(M, 1024)
    return pl.pallas_call(
        _bmm_kernel,
        out_shape=jax.ShapeDtypeStruct((B, M, N), F32),
        grid=(B, M // tm),
        in_specs=[pl.BlockSpec((None, tm, K), lambda b, i: (b, i, 0)),
                  pl.BlockSpec((None, K, N), lambda b, i: (b, 0, 0))],
        out_specs=pl.BlockSpec((None, tm, N), lambda b, i: (b, i, 0)),
        compiler_params=_cparams(("parallel", "arbitrary")),
        name=name,
    )(x, w)


def _block_mean_kernel(k_ref, o_ref, *, inv):
    o_ref[...] = jnp.sum(k_ref[...], axis=0, keepdims=True) * inv


def block_mean(k, blk):
    B, L, D = k.shape
    nb = L // blk
    out = pl.pallas_call(
        functools.partial(_block_mean_kernel, inv=1.0 / blk),
        out_shape=jax.ShapeDtypeStruct((B, nb, 1, D), F32),
        grid=(B, nb),
        in_specs=[pl.BlockSpec((None, blk, D), lambda b, n: (b, n, 0))],
        out_specs=pl.BlockSpec((None, None, 1, D), lambda b, n: (b, n, 0, 0)),
        compiler_params=_cparams(("parallel", "arbitrary")),
        name="block_mean",
    )(k)
    return out.reshape(B, nb, D)


def _flash_kernel(*refs, G, slopes, kv_heads, qpos0, kpos0, lk, window, sel_blk, scale, tr, tk, nkt):
    if sel_blk is None:
        q_ref, k_ref, v_ref, o_ref, m_sc, l_sc, acc_sc = refs
        sel_ref = None
    else:
        q_ref, k_ref, v_ref, sel_ref, o_ref, m_sc, l_sc, acc_sc = refs
    b = pl.program_id(0)
    i = pl.program_id(1)
    j = pl.program_id(2)
    shift = int(math.log2(G))

    @pl.when(j == 0)
    def _():
        m_sc[...] = jnp.full_like(m_sc, NEG_INF)
        l_sc[...] = jnp.zeros_like(l_sc)
        acc_sc[...] = jnp.zeros_like(acc_sc)

    q_lo = qpos0 + lax.shift_right_logical(i * tr, shift)
    q_hi = qpos0 + lax.shift_right_logical(i * tr + (tr - 1), shift)
    k_lo = kpos0 + j * tk
    needed = jnp.logical_and(k_lo <= q_hi, j * tk < lk)
    if window is not None:
        needed = jnp.logical_and(needed, k_lo + (tk - 1) > q_lo - window)

    @pl.when(needed)
    def _():
        q = q_ref[...].astype(BF16)
        k = k_ref[...].astype(BF16)
        s = lax.dot_general(q, k, (((1,), (1,)), ((), ())), preferred_element_type=F32) * scale
        rows = i * tr + lax.broadcasted_iota(jnp.int32, (tr, 1), 0)
        qpos = qpos0 + lax.shift_right_logical(rows, shift)
        kidx = j * tk + lax.broadcasted_iota(jnp.int32, (1, tk), 1)
        dist = qpos - (kpos0 + kidx)
        mask = dist >= 0
        if lk < nkt * tk:
            mask = jnp.logical_and(mask, kidx < lk)
        if window is not None:
            mask = jnp.logical_and(mask, dist < window)
        if slopes is not None:
            g = jnp.bitwise_and(rows, G - 1)
            head = g if kv_heads == 1 else g + G * lax.rem(b, kv_heads)
            s = s - _row_slopes(head, slopes, (tr, 1)) * dist.astype(F32)
        if sel_ref is not None:
            nbp = sel_ref.shape[-1]
            blk_of_key = lax.shift_right_logical(kidx, int(math.log2(sel_blk)))
            expand = (lax.broadcasted_iota(jnp.int32, (nbp, tk), 0) == blk_of_key).astype(BF16)
            picked = jnp.dot(sel_ref[...].astype(BF16), expand, preferred_element_type=F32)
            mask = jnp.logical_and(mask, picked > 0.5)
        s = jnp.where(mask, s, NEG_INF)
        m_prev = m_sc[...]
        m_new = jnp.maximum(m_prev, jnp.max(s, axis=-1, keepdims=True))
        alpha = jnp.exp(m_prev - m_new)
        p = jnp.where(mask, jnp.exp(s - m_new), 0.0)
        l_sc[...] = alpha * l_sc[...] + jnp.sum(p, axis=-1, keepdims=True)
        acc_sc[...] = alpha * acc_sc[...] + jnp.dot(p.astype(BF16), v_ref[...].astype(BF16),
                                                    preferred_element_type=F32)
        m_sc[...] = m_new

    @pl.when(j == nkt - 1)
    def _():
        o_ref[...] = acc_sc[...] / jnp.maximum(l_sc[...], 1e-30)


def flash(q, k, v, *, G, slopes, kv_heads=1, qpos0, kpos0=0, lk, window=None, sel=None, sel_blk=None,
          scale, tr, tk, name):
    B, R, Dk = q.shape
    Lk, Dv = v.shape[1], v.shape[2]
    tr = _pick_tile(R, tr)
    assert Lk % tk == 0 and G & (G - 1) == 0
    nkt = Lk // tk
    kern = functools.partial(_flash_kernel, G=G, slopes=slopes, kv_heads=kv_heads, qpos0=qpos0, kpos0=kpos0,
                             lk=lk, window=window, sel_blk=sel_blk, scale=scale, tr=tr, tk=tk, nkt=nkt)
    in_specs = [pl.BlockSpec((None, tr, Dk), lambda b, i, j: (b, i, 0)),
                pl.BlockSpec((None, tk, Dk), lambda b, i, j: (b, j, 0)),
                pl.BlockSpec((None, tk, Dv), lambda b, i, j: (b, j, 0))]
    args = [q, k, v]
    if sel is not None:
        in_specs.append(pl.BlockSpec((None, tr, sel.shape[-1]), lambda b, i, j: (b, i, 0)))
        args.append(sel)
    return pl.pallas_call(
        kern,
        out_shape=jax.ShapeDtypeStruct((B, R, Dv), F32),
        grid=(B, R // tr, nkt),
        in_specs=in_specs,
        out_specs=pl.BlockSpec((None, tr, Dv), lambda b, i, j: (b, i, 0)),
        scratch_shapes=[pltpu.VMEM((tr, 1), F32), pltpu.VMEM((tr, 1), F32), pltpu.VMEM((tr, Dv), F32)],
        compiler_params=_cparams(("parallel", "parallel", "arbitrary")),
        name=name,
    )(*args)


def _gelu_proj_kernel(a_ref, b_ref, bias_ref, w_ref, o_ref):
    pre = a_ref[...] + b_ref[...] + bias_ref[...]
    act = jax.nn.gelu(pre)
    o_ref[...] = jnp.dot(act.astype(BF16), w_ref[...].astype(BF16), preferred_element_type=F32)


def gelu_proj(a, b, bias, w):
    M, Fd = a.shape
    D = w.shape[1]
    tm = _pick_tile(M, 1024)
    return pl.pallas_call(
        _gelu_proj_kernel,
        out_shape=jax.ShapeDtypeStruct((M, D), F32),
        grid=(M // tm,),
        in_specs=[pl.BlockSpec((tm, Fd), lambda i: (i, 0)),
                  pl.BlockSpec((tm, Fd), lambda i: (i, 0)),
                  pl.BlockSpec((1, Fd), lambda i: (0, 0)),
                  pl.BlockSpec((Fd, D), lambda i: (0, 0))],
        out_specs=pl.BlockSpec((tm, D), lambda i: (i, 0)),
        compiler_params=_cparams(("parallel",)),
        name="gelu_proj",
    )(a, b, bias, w)


def _cmp_attn_kernel(q_ref, ck_ref, cv_ref, ovn_ref, o_ref, imp_ref, *, H, slopes, qpos0, n_cmp, scale, tr):
    i = pl.program_id(1)
    ncp = ck_ref.shape[0]
    shift = int(math.log2(H))
    q = q_ref[...].astype(BF16)
    ck = ck_ref[...].astype(BF16)
    s = lax.dot_general(q, ck, (((1,), (1,)), ((), ())), preferred_element_type=F32) * scale
    rows = i * tr + lax.broadcasted_iota(jnp.int32, (tr, 1), 0)
    qpos = qpos0 + lax.shift_right_logical(rows, shift)
    n_idx = lax.broadcasted_iota(jnp.int32, (1, ncp), 1)
    dist = qpos - (n_idx * NSA_CMP_STRIDE + (NSA_CMP_LEN - 1))
    mask = jnp.logical_and(dist >= 0, n_idx < n_cmp)
    head = jnp.bitwise_and(rows, H - 1)
    slope = jnp.zeros((tr, 1), F32)
    for h, sl in enumerate(slopes):
        slope = jnp.where(head == h, np.float32(sl), slope)
    s = s - slope * dist.astype(F32)
    s = jnp.where(mask, s, NEG_INF)
    m = jnp.max(s, axis=-1, keepdims=True)
    e = jnp.where(mask, jnp.exp(s - m), 0.0)
    p = e / jnp.maximum(jnp.sum(e, axis=-1, keepdims=True), 1e-30)
    o_ref[...] = jnp.dot(p.astype(BF16), cv_ref[...].astype(BF16), preferred_element_type=F32)
    tq = tr // H
    head_sum = (lax.shift_right_logical(lax.broadcasted_iota(jnp.int32, (tq, tr), 1), shift)
                == lax.broadcasted_iota(jnp.int32, (tq, tr), 0)).astype(F32)
    p_sum = jnp.dot(head_sum, p, preferred_element_type=F32, precision=HIGHEST)
    imp_ref[...] = jnp.dot(p_sum, ovn_ref[...], preferred_element_type=F32, precision=HIGHEST)


def cmp_attention(q, ck, cv, ovn, *, H, slopes, qpos0, n_cmp, scale, tr):
    B, R, D = q.shape
    ncp = ck.shape[1]
    nsp = ovn.shape[1]
    tr = _pick_tile(R, tr)
    return pl.pallas_call(
        functools.partial(_cmp_attn_kernel, H=H, slopes=slopes, qpos0=qpos0, n_cmp=n_cmp, scale=scale, tr=tr),
        out_shape=(jax.ShapeDtypeStruct((B, R, D), F32), jax.ShapeDtypeStruct((B, R // H, nsp), F32)),
        grid=(B, R // tr),
        in_specs=[pl.BlockSpec((None, tr, D), lambda b, i: (b, i, 0)),
                  pl.BlockSpec((None, ncp, D), lambda b, i: (b, 0, 0)),
                  pl.BlockSpec((None, ncp, D), lambda b, i: (b, 0, 0)),
                  pl.BlockSpec((ncp, nsp), lambda b, i: (0, 0))],
        out_specs=(pl.BlockSpec((None, tr, D), lambda b, i: (b, i, 0)),
                   pl.BlockSpec((None, tr // H, nsp), lambda b, i: (b, i, 0))),
        compiler_params=_cparams(("parallel", "arbitrary")),
        name="nsa_cmp_attn",
    )(q, ck, cv, ovn)


PAGES_PER_STEP = 16


def _page_specs(rows, row_block, layer, n_pages):
    def spec(p):
        return pl.BlockSpec((None, None, rows, LANE),
                            lambda b, j, tbl: (layer, tbl[b * n_pages + j * PAGES_PER_STEP + p], row_block, 0))
    return [spec(p) for p in range(PAGES_PER_STEP)]


def _row_slopes(head, slopes, shape):
    slope = jnp.zeros(shape, F32)
    for h, sl in enumerate(slopes):
        slope = jnp.where(head == h, np.float32(sl), slope)
    return slope


def _nt_dot(a, b):
    return lax.dot_general(a, b, (((1,), (1,)), ((), ())), preferred_element_type=F32)


def _moba_paged_kernel(tbl_ref, q_ref, new_ref, *rest, slopes, past_len, n_steps, T, G, scale):
    pages = rest[:PAGES_PER_STEP]
    o_ref, m_part, l_part, kmean, o_part = rest[PAGES_PER_STEP:]
    j = pl.program_id(1)
    KVH = MOBA_KV_HEADS
    R = T * G
    HD = HEAD_DIM
    blocks_per_step = PAGES_PER_STEP * LANE // MOBA_BLOCK
    pages_per_block = MOBA_BLOCK // LANE
    nb_past = past_len // MOBA_BLOCK
    lane = lax.broadcasted_iota(jnp.int32, (1, LANE), 1)
    rows = lax.broadcasted_iota(jnp.int32, (R, 1), 0)
    t_of_row = lax.shift_right_logical(rows, int(math.log2(G)))
    qpos = past_len + t_of_row

    @pl.when(j == 0)
    def _():
        m_part[...] = jnp.full_like(m_part, NEG_INF)
        l_part[...] = jnp.zeros_like(l_part)
        kmean[...] = jnp.zeros_like(kmean)

    for kvh in range(KVH):
        rs = pl.ds(kvh * R, R)
        slope = _row_slopes(jnp.bitwise_and(rows, G - 1) + kvh * G, slopes, (R, 1))
        qf = q_ref[rs, :]
        qb = qf.astype(BF16)
        m_acc = m_part[rs, :]
        l_acc = l_part[rs, :]
        km_acc = kmean[kvh]
        for c in range(blocks_per_step):
            pg = [pages[c * pages_per_block + u] for u in range(pages_per_block)]
            kt = jnp.concatenate([p_[pl.ds(kvh * 2 * HD, HD), :] for p_ in pg], axis=1)
            vt = jnp.concatenate([p_[pl.ds(kvh * 2 * HD + HD, HD), :] for p_ in pg], axis=1)
            blk = j * blocks_per_step + c
            kpos = blk * MOBA_BLOCK + lax.broadcasted_iota(jnp.int32, (1, MOBA_BLOCK), 1)
            s = jnp.dot(qb, kt.astype(BF16), preferred_element_type=F32) * scale
            s = s - slope * (qpos - kpos).astype(F32)
            m = jnp.max(s, axis=-1, keepdims=True)
            p = jnp.exp(s - m)
            here = lane == blk
            m_acc = jnp.where(here, m, m_acc)
            l_acc = jnp.where(here, jnp.sum(p, axis=-1, keepdims=True), l_acc)
            o_part[blk, rs, :] = _nt_dot(p.astype(BF16), vt.astype(BF16))
            km_acc = jnp.where(here, jnp.sum(kt, axis=-1, keepdims=True) * (1.0 / MOBA_BLOCK), km_acc)
        m_part[rs, :] = m_acc
        l_part[rs, :] = l_acc
        kmean[kvh] = km_acc

        @pl.when(j == n_steps - 1)
        def _():
            gate = jnp.dot(qf, kmean[kvh], preferred_element_type=F32, precision=HIGHEST)
            past = lane < nb_past
            gate = jnp.where(past, gate, NEG_INF)
            rank = jnp.zeros((R, LANE), jnp.int32)
            for c2 in range(nb_past):
                col = gate[:, c2:c2 + 1]
                beats = jnp.logical_or(col > gate, jnp.logical_and(col == gate, lane > c2))
                rank = rank + beats.astype(jnp.int32)
            sel = jnp.logical_and(past, rank < MOBA_TOPK)
            k_new = new_ref[:, pl.ds(kvh * 2 * HD, HD)]
            v_new = new_ref[:, pl.ds(kvh * 2 * HD + HD, HD)]
            dist = t_of_row - lax.broadcasted_iota(jnp.int32, (1, T), 1)
            ok = dist >= 0
            s_own = _nt_dot(qb, k_new.astype(BF16)) * scale - slope * dist.astype(F32)
            s_own = jnp.where(ok, s_own, NEG_INF)
            mp = m_part[rs, :]
            m_all = jnp.maximum(jnp.max(s_own, axis=-1, keepdims=True),
                                jnp.max(jnp.where(sel, mp, NEG_INF), axis=-1, keepdims=True))
            p_own = jnp.where(ok, jnp.exp(s_own - m_all), 0.0)
            w = jnp.where(sel, jnp.exp(mp - m_all), 0.0)
            l_all = jnp.sum(p_own, axis=-1, keepdims=True) + jnp.sum(w * l_part[rs, :], axis=-1, keepdims=True)
            acc = jnp.dot(p_own.astype(BF16), v_new.astype(BF16), preferred_element_type=F32)
            for c2 in range(nb_past):
                acc = acc + w[:, c2:c2 + 1] * o_part[c2, rs, :]
            o_ref[rs, :] = acc / jnp.maximum(l_all, 1e-30)


def moba_paged(q, proj, cache_t, table, layer, *, past_len, T):
    B, R2, HD = q.shape
    n_pages = past_len // LANE
    n_steps = n_pages // PAGES_PER_STEP
    G = MOBA_HEADS // MOBA_KV_HEADS
    nb_past = past_len // MOBA_BLOCK
    assert n_pages % PAGES_PER_STEP == 0 and past_len % MOBA_BLOCK == 0 and T <= MOBA_BLOCK and nb_past <= LANE
    width = MOBA_KV_HEADS * 2 * HD
    kern = functools.partial(_moba_paged_kernel, slopes=_alibi_slopes(MOBA_HEADS), past_len=past_len,
                             n_steps=n_steps, T=T, G=G, scale=HD ** -0.5)
    return pl.pallas_call(
        kern,
        out_shape=jax.ShapeDtypeStruct((B, R2, HD), F32),
        grid_spec=pltpu.PrefetchScalarGridSpec(
            num_scalar_prefetch=1, grid=(B, n_steps),
            in_specs=[pl.BlockSpec((None, R2, HD), lambda b, j, tbl: (b, 0, 0)),
                      pl.BlockSpec((T, width), lambda b, j, tbl: (b, C_MKV // width))]
            + _page_specs(width, 0, layer, n_pages),
            out_specs=pl.BlockSpec((None, R2, HD), lambda b, j, tbl: (b, 0, 0)),
            scratch_shapes=[pltpu.VMEM((R2, LANE), F32), pltpu.VMEM((R2, LANE), F32),
                            pltpu.VMEM((MOBA_KV_HEADS, HD, LANE), F32), pltpu.VMEM((nb_past, R2, HD), F32)]),
        compiler_params=_cparams(("parallel", "arbitrary")),
        name="moba_paged",
    )(table, q, proj, *([cache_t] * PAGES_PER_STEP))


def _online_update(s, mask, v_dot, m_sc, l_sc, acc_sc):
    m_prev = m_sc[...]
    m_new = jnp.maximum(m_prev, jnp.max(s, axis=-1, keepdims=True))
    alpha = jnp.exp(m_prev - m_new)
    p = jnp.exp(s - m_new)
    if mask is not None:
        p = jnp.where(mask, p, 0.0)
    l_sc[...] = alpha * l_sc[...] + jnp.sum(p, axis=-1, keepdims=True)
    acc_sc[...] = alpha * acc_sc[...] + v_dot(p.astype(BF16))
    m_sc[...] = m_new


def _flash_init(j, m_sc, l_sc, acc_sc):
    @pl.when(j == 0)
    def _():
        m_sc[...] = jnp.full_like(m_sc, NEG_INF)
        l_sc[...] = jnp.zeros_like(l_sc)
        acc_sc[...] = jnp.zeros_like(acc_sc)


def _mla_paged_kernel(tbl_ref, q_ref, new_ref, *rest, n_steps, T, H, scale):
    pages = rest[:PAGES_PER_STEP]
    o_ref, m_sc, l_sc, acc_sc = rest[PAGES_PER_STEP:]
    j = pl.program_id(1)
    C = MLA_KV_LORA
    R = T * H
    _flash_init(j, m_sc, l_sc, acc_sc)
    qb = q_ref[...].astype(BF16)
    pg = [p_[...].astype(BF16) for p_ in pages]
    s = jnp.concatenate([jnp.dot(qb, x, preferred_element_type=F32) for x in pg], axis=1) * scale

    def v_dot(p):
        out = None
        for u, x in enumerate(pg):
            term = _nt_dot(p[:, u * LANE:(u + 1) * LANE], x[:C, :])
            out = term if out is None else out + term
        return out

    _online_update(s, None, v_dot, m_sc, l_sc, acc_sc)

    @pl.when(j == n_steps - 1)
    def _():
        new = new_ref[...].astype(BF16)
        rows = lax.broadcasted_iota(jnp.int32, (R, 1), 0)
        ok = lax.shift_right_logical(rows, int(math.log2(H))) >= lax.broadcasted_iota(jnp.int32, (1, T), 1)
        s_new = jnp.where(ok, _nt_dot(qb, new) * scale, NEG_INF)
        _online_update(s_new, ok, lambda p: jnp.dot(p, new[:, :C], preferred_element_type=F32), m_sc, l_sc, acc_sc)
        o_ref[...] = acc_sc[...] / jnp.maximum(l_sc[...], 1e-30)


def mla_paged(q_abs, new_rows, cache_t, table, layer, *, past_len, T):
    B, R, Dk = q_abs.shape
    n_pages = past_len // LANE
    n_steps = n_pages // PAGES_PER_STEP
    assert n_pages % PAGES_PER_STEP == 0
    kern = functools.partial(_mla_paged_kernel, n_steps=n_steps, T=T, H=MLA_HEADS,
                             scale=(MLA_NOPE + MLA_ROPE) ** -0.5)
    return pl.pallas_call(
        kern,
        out_shape=jax.ShapeDtypeStruct((B, R, MLA_KV_LORA), F32),
        grid_spec=pltpu.PrefetchScalarGridSpec(
            num_scalar_prefetch=1, grid=(B, n_steps),
            in_specs=[pl.BlockSpec((None, R, Dk), lambda b, j, tbl: (b, 0, 0)),
                      pl.BlockSpec((None, T, Dk), lambda b, j, tbl: (b, 0, 0))]
            + _page_specs(Dk, 0, layer, n_pages),
            out_specs=pl.BlockSpec((None, R, MLA_KV_LORA), lambda b, j, tbl: (b, 0, 0)),
            scratch_shapes=[pltpu.VMEM((R, 1), F32), pltpu.VMEM((R, 1), F32), pltpu.VMEM((R, MLA_KV_LORA), F32)]),
        compiler_params=_cparams(("parallel", "arbitrary")),
        name="mla_paged",
    )(table, q_abs, new_rows, *([cache_t] * PAGES_PER_STEP))


def _nsa_cmp_paged_kernel(tbl_ref, q_ref, w1_ref, bias_ref, w2_ref, ovn_ref, *rest, slopes, past_len, n_steps, T, H,
                          n_cmp, n_sel, scale):
    pages = rest[:PAGES_PER_STEP]
    o_ref, sel_ref, xt_sc, part_sc = rest[PAGES_PER_STEP:]
    j = pl.program_id(1)
    HD, FH = HEAD_DIM, NSA_CMP_HIDDEN
    blocks_per_step = PAGES_PER_STEP * LANE // NSA_CMP_STRIDE
    m_blocks = n_steps * blocks_per_step
    R = T * H

    @pl.when(j == 0)
    def _():
        part_sc[pl.ds(m_blocks, 8), :] = jnp.zeros((8, part_sc.shape[1]), F32)

    for u, p_ in enumerate(pages):
        xt_sc[pl.ds(u * LANE, LANE), :] = p_[...].T
    acc = jnp.zeros((blocks_per_step, part_sc.shape[1]), F32)
    for jj in range(NSA_CMP_STRIDE):
        x = xt_sc[pl.ds(jj, blocks_per_step, stride=NSA_CMP_STRIDE), :]
        acc = acc + jnp.dot(x.astype(BF16), w1_ref[jj], preferred_element_type=F32)
    part_sc[pl.ds(pl.multiple_of(j * blocks_per_step, blocks_per_step), blocks_per_step), :] = acc

    @pl.when(j == n_steps - 1)
    def _():
        comp = []
        for kk in range(2):
            a0 = part_sc[pl.ds(0, m_blocks), pl.ds(kk * 2 * FH, FH)]
            a1 = part_sc[pl.ds(1, m_blocks), pl.ds(kk * 2 * FH + FH, FH)]
            act = jax.nn.gelu(a0 + a1 + bias_ref[:, pl.ds(kk * FH, FH)])
            comp.append(jnp.dot(act.astype(BF16), w2_ref[kk].astype(BF16), preferred_element_type=F32))
        qb = q_ref[...].astype(BF16)
        rows = lax.broadcasted_iota(jnp.int32, (R, 1), 0)
        shift = int(math.log2(H))
        qpos = past_len + lax.shift_right_logical(rows, shift)
        n_idx = lax.broadcasted_iota(jnp.int32, (1, m_blocks), 1)
        dist = qpos - (n_idx * NSA_CMP_STRIDE + (NSA_CMP_LEN - 1))
        mask = jnp.logical_and(dist >= 0, n_idx < n_cmp)
        slope = _row_slopes(jnp.bitwise_and(rows, H - 1), slopes, (R, 1))
        s = _nt_dot(qb, comp[0].astype(BF16)) * scale - slope * dist.astype(F32)
        s = jnp.where(mask, s, NEG_INF)
        e = jnp.where(mask, jnp.exp(s - jnp.max(s, axis=-1, keepdims=True)), 0.0)
        p = e / jnp.maximum(jnp.sum(e, axis=-1, keepdims=True), 1e-30)
        o_ref[...] = jnp.dot(p.astype(BF16), comp[1].astype(BF16), preferred_element_type=F32)
        head_sum = (lax.shift_right_logical(lax.broadcasted_iota(jnp.int32, (T, R), 1), shift)
                    == lax.broadcasted_iota(jnp.int32, (T, R), 0)).astype(F32)
        p_sum = jnp.dot(head_sum, p, preferred_element_type=F32, precision=HIGHEST)
        imp = jnp.dot(p_sum, ovn_ref[...], preferred_element_type=F32, precision=HIGHEST)
        nsp = imp.shape[1]
        lane = lax.broadcasted_iota(jnp.int32, (1, nsp), 1)
        own = lax.shift_right_logical(past_len + lax.broadcasted_iota(jnp.int32, (T, 1), 0),
                                      int(math.log2(NSA_SEL_BLOCK)))
        forced = jnp.logical_or(lane == 0, jnp.logical_or(lane == own, lane == own - 1))
        valid = jnp.logical_and(lane <= own, lane < n_sel)
        score = jnp.where(valid, imp + jnp.where(forced, FORCE_BONUS, 0.0), NEG_INF)
        rank = jnp.zeros((T, nsp), jnp.int32)
        for c2 in range(n_sel):
            col = score[:, c2:c2 + 1]
            beats = jnp.logical_or(col > score, jnp.logical_and(col == score, lane > c2))
            rank = rank + beats.astype(jnp.int32)
        sel_ref[...] = jnp.logical_and(valid, rank < NSA_TOPN).astype(F32)


def nsa_cmp_paged(q, w1bd, bias2, w2, ovn, cache_t, table, layer, *, past_len, T, n_cmp, n_sel):
    B, R, HD = q.shape
    n_pages = past_len // LANE
    n_steps = n_pages // PAGES_PER_STEP
    assert n_pages % PAGES_PER_STEP == 0
    m_blocks = past_len // NSA_CMP_STRIDE
    nsp = ovn.shape[1]
    pw = w1bd.shape[2]
    kern = functools.partial(_nsa_cmp_paged_kernel, slopes=_alibi_slopes(NSA_HEADS), past_len=past_len,
                             n_steps=n_steps, T=T, H=NSA_HEADS, n_cmp=n_cmp, n_sel=n_sel, scale=HD ** -0.5)
    return pl.pallas_call(
        kern,
        out_shape=(jax.ShapeDtypeStruct((B, R, HD), F32), jax.ShapeDtypeStruct((B, T, nsp), F32)),
        grid_spec=pltpu.PrefetchScalarGridSpec(
            num_scalar_prefetch=1, grid=(B, n_steps),
            in_specs=[pl.BlockSpec((None, R, HD), lambda b, j, tbl: (b, 0, 0)),
                      pl.BlockSpec(w1bd.shape, lambda b, j, tbl: (0, 0, 0)),
                      pl.BlockSpec(bias2.shape, lambda b, j, tbl: (0, 0)),
                      pl.BlockSpec(w2.shape, lambda b, j, tbl: (0, 0, 0)),
                      pl.BlockSpec(ovn.shape, lambda b, j, tbl: (0, 0))]
            + _page_specs(2 * HD, 0, layer, n_pages),
            out_specs=(pl.BlockSpec((None, R, HD), lambda b, j, tbl: (b, 0, 0)),
                       pl.BlockSpec((None, T, nsp), lambda b, j, tbl: (b, 0, 0))),
            scratch_shapes=[pltpu.VMEM((PAGES_PER_STEP * LANE, 2 * HD), F32), pltpu.VMEM((m_blocks + 8, pw), F32)]),
        compiler_params=_cparams(("parallel", "arbitrary")),
        name="nsa_cmp_paged",
    )(table, q, w1bd, bias2, w2, ovn, *([cache_t] * PAGES_PER_STEP))


def _nsa_sel_paged_kernel(tbl_ref, q_ref, selx_ref, selown_ref, knew_ref, wnew_ref, win_ref, *rest, slopes,
                          past_len, n_steps, T, H, scale):
    pages = rest[:PAGES_PER_STEP]
    osel_ref, owin_ref, m_sc, l_sc, acc_sc = rest[PAGES_PER_STEP:]
    j = pl.program_id(1)
    HD = HEAD_DIM
    R = T * H
    K = PAGES_PER_STEP * LANE
    blocks_per_step = K // NSA_SEL_BLOCK
    _flash_init(j, m_sc, l_sc, acc_sc)
    rows = lax.broadcasted_iota(jnp.int32, (R, 1), 0)
    t_of_row = lax.shift_right_logical(rows, int(math.log2(H)))
    qpos = past_len + t_of_row
    slope = _row_slopes(jnp.bitwise_and(rows, H - 1), slopes, (R, 1))
    qb = q_ref[...].astype(BF16)

    pg = [p_[...].astype(BF16) for p_ in pages]
    s = jnp.concatenate([jnp.dot(qb, x[:HD, :], preferred_element_type=F32) for x in pg], axis=1) * scale
    kidx = lax.broadcasted_iota(jnp.int32, (1, K), 1)
    s = s - slope * (qpos - (j * K + kidx)).astype(F32)
    expand = (lax.broadcasted_iota(jnp.int32, (blocks_per_step, K), 0)
              == lax.shift_right_logical(lax.broadcasted_iota(jnp.int32, (blocks_per_step, K), 1),
                                         int(math.log2(NSA_SEL_BLOCK)))).astype(BF16)
    picked = jnp.dot(selx_ref[...].astype(BF16), expand, preferred_element_type=F32) > 0.5
    s = jnp.where(picked, s, NEG_INF)

    def v_dot(p):
        out = None
        for u, x in enumerate(pg):
            term = _nt_dot(p[:, u * LANE:(u + 1) * LANE], x[HD:, :])
            out = term if out is None else out + term
        return out

    _online_update(s, picked, v_dot, m_sc, l_sc, acc_sc)

    dist_new = t_of_row - lax.broadcasted_iota(jnp.int32, (1, T), 1)

    @pl.when(j == 0)
    def _():
        wb = win_ref.shape[1]
        win = win_ref[...].astype(BF16)
        wnew = wnew_ref[...].astype(BF16)
        dist_w = (qpos - (past_len - wb)) - lax.broadcasted_iota(jnp.int32, (1, wb), 1)
        ok_w = jnp.logical_and(dist_w >= 0, dist_w < NSA_WINDOW)
        ok_n = jnp.logical_and(dist_new >= 0, dist_new < NSA_WINDOW)
        s_w = jnp.dot(qb, win[:HD, :], preferred_element_type=F32) * scale - slope * dist_w.astype(F32)
        s_n = _nt_dot(qb, wnew[:, :HD]) * scale - slope * dist_new.astype(F32)
        s_w = jnp.where(ok_w, s_w, NEG_INF)
        s_n = jnp.where(ok_n, s_n, NEG_INF)
        m = jnp.maximum(jnp.max(s_w, axis=-1, keepdims=True), jnp.max(s_n, axis=-1, keepdims=True))
        p_w = jnp.where(ok_w, jnp.exp(s_w - m), 0.0)
        p_n = jnp.where(ok_n, jnp.exp(s_n - m), 0.0)
        l = jnp.sum(p_w, axis=-1, keepdims=True) + jnp.sum(p_n, axis=-1, keepdims=True)
        o = _nt_dot(p_w.astype(BF16), win[HD:, :]) + jnp.dot(p_n.astype(BF16), wnew[:, HD:],
                                                             preferred_element_type=F32)
        owin_ref[...] = o / jnp.maximum(l, 1e-30)

    @pl.when(j == n_steps - 1)
    def _():
        knew = knew_ref[...].astype(BF16)
        ok = jnp.logical_and(dist_new >= 0, selown_ref[...] > 0.5)
        s_new = _nt_dot(qb, knew[:, :HD]) * scale - slope * dist_new.astype(F32)
        s_new = jnp.where(ok, s_new, NEG_INF)
        _online_update(s_new, ok, lambda p: jnp.dot(p, knew[:, HD:], preferred_element_type=F32), m_sc, l_sc, acc_sc)
        osel_ref[...] = acc_sc[...] / jnp.maximum(l_sc[...], 1e-30)


def nsa_sel_paged(q, selx, selown, proj, win_t, cache_t, table, layer, *, past_len, T):
    B, R, HD = q.shape
    n_pages = past_len // LANE
    n_steps = n_pages // PAGES_PER_STEP
    wb = win_t.shape[3]
    bps = selx.shape[3]
    kern = functools.partial(_nsa_sel_paged_kernel, slopes=_alibi_slopes(NSA_HEADS), past_len=past_len,
                             n_steps=n_steps, T=T, H=NSA_HEADS, scale=HD ** -0.5)
    out = jax.ShapeDtypeStruct((B, R, HD), F32)
    o_spec = pl.BlockSpec((None, R, HD), lambda b, j, tbl: (b, 0, 0))
    return pl.pallas_call(
        kern,
        out_shape=(out, out),
        grid_spec=pltpu.PrefetchScalarGridSpec(
            num_scalar_prefetch=1, grid=(B, n_steps),
            in_specs=[pl.BlockSpec((None, R, HD), lambda b, j, tbl: (b, 0, 0)),
                      pl.BlockSpec((None, None, R, bps), lambda b, j, tbl: (b, j, 0, 0)),
                      pl.BlockSpec((None, R, 1), lambda b, j, tbl: (b, 0, 0)),
                      pl.BlockSpec((T, 2 * HD), lambda b, j, tbl: (b, (C_NKV + 2 * HD) // (2 * HD))),
                      pl.BlockSpec((T, 2 * HD), lambda b, j, tbl: (b, (C_NKV + 4 * HD) // (2 * HD))),
                      pl.BlockSpec((None, None, 2 * HD, wb), lambda b, j, tbl: (layer, b, 0, 0))]
            + _page_specs(2 * HD, 1, layer, n_pages),
            out_specs=(o_spec, o_spec),
            scratch_shapes=[pltpu.VMEM((R, 1), F32), pltpu.VMEM((R, 1), F32), pltpu.VMEM((R, HD), F32)]),
        compiler_params=_cparams(("parallel", "arbitrary")),
        name="nsa_sel_paged",
    )(table, q, selx, selown, proj, proj, win_t, *([cache_t] * PAGES_PER_STEP))


def _merge_kernel(oa_ref, ob_ref, oc_ref, wb_ref, ga_ref, gb_ref, gc_ref, o_ref):
    acc = None
    for n, (o_r, g_r) in enumerate(((oa_ref, ga_ref), (ob_ref, gb_ref), (oc_ref, gc_ref))):
        up = jnp.dot(o_r[...].astype(BF16), wb_ref[n], preferred_element_type=F32)
        term = jax.nn.sigmoid(g_r[...]) * up
        acc = term if acc is None else acc + term
    o_ref[...] = acc.astype(o_ref.dtype)


def merge_branches(o_a, o_b, o_c, w_branch, proj, d_model, *, tm=512, tn=512):
    M = o_a.shape[0]
    tm = _pick_tile(M, tm)
    tn = _pick_tile(d_model, tn)
    nj = d_model // tn
    goff = C_MGATE // tn

    def gate_spec(n):
        return pl.BlockSpec((tm, tn), lambda i, j: (i, goff + n * nj + j))

    o_spec = pl.BlockSpec((tm, BRANCH_WIDTH), lambda i, j: (i, 0))
    return pl.pallas_call(
        _merge_kernel,
        out_shape=jax.ShapeDtypeStruct((M, d_model), BF16),
        grid=(M // tm, nj),
        in_specs=[o_spec, o_spec, o_spec,
                  pl.BlockSpec((N_BRANCH, BRANCH_WIDTH, tn), lambda i, j: (0, 0, j)),
                  gate_spec(0), gate_spec(1), gate_spec(2)],
        out_specs=pl.BlockSpec((tm, tn), lambda i, j: (i, j)),
        compiler_params=_cparams(("parallel", "arbitrary")),
        name="merge_branches",
    )(o_a, o_b, o_c, w_branch, proj, proj, proj)


def _layer_norm_rows(v, g, b):
    mu = jnp.mean(v, axis=-1, keepdims=True)
    var = jnp.mean(jnp.square(v - mu), axis=-1, keepdims=True)
    return (v - mu) * lax.rsqrt(var + LN_EPS) * g + b


def _out_ln_kernel(m_ref, w_ref, x_ref, g_ref, b_ref, o_ref, *, alpha):
    mix = jnp.dot(m_ref[...].astype(BF16), w_ref[...], preferred_element_type=F32)
    o_ref[...] = _layer_norm_rows(alpha * x_ref[...] + mix, g_ref[...], b_ref[...])


def out_proj_ln(merged, w_out, x, g, b, *, alpha, tm=256):
    M, D = x.shape
    tm = _pick_tile(M, tm)
    return pl.pallas_call(
        functools.partial(_out_ln_kernel, alpha=alpha),
        out_shape=jax.ShapeDtypeStruct((M, D), F32),
        grid=(M // tm,),
        in_specs=[pl.BlockSpec((tm, D), lambda i: (i, 0)),
                  pl.BlockSpec((D, D), lambda i: (0, 0)),
                  pl.BlockSpec((tm, D), lambda i: (i, 0)),
                  pl.BlockSpec((1, D), lambda i: (0, 0)),
                  pl.BlockSpec((1, D), lambda i: (0, 0))],
        out_specs=pl.BlockSpec((tm, D), lambda i: (i, 0)),
        compiler_params=_cparams(("parallel",)),
        name="out_proj_ln1",
    )(merged, w_out, x, g, b)


def _moe_up_kernel(te_ref, nt_ref, xs_ref, wg_ref, wu_ref, rw_ref, o_ref):
    i = pl.program_id(0)

    @pl.when(i < nt_ref[0])
    def _():
        x = xs_ref[...]
        gate = jnp.dot(x, wg_ref[...], preferred_element_type=F32)
        up = jnp.dot(x, wu_ref[...], preferred_element_type=F32)
        o_ref[...] = (jax.nn.silu(gate) * up * rw_ref[...]).astype(o_ref.dtype)

    @pl.when(i >= nt_ref[0])
    def _():
        o_ref[...] = jnp.zeros_like(o_ref)


def _moe_down_kernel(te_ref, nt_ref, h_ref, wd_ref, o_ref):
    i = pl.program_id(0)

    @pl.when(i < nt_ref[0])
    def _():
        o_ref[...] = jnp.dot(h_ref[...], wd_ref[...], preferred_element_type=F32)

    @pl.when(i >= nt_ref[0])
    def _():
        o_ref[...] = jnp.zeros_like(o_ref)


def moe_experts(xs, row_w, tile_expert, n_tiles, w_gate, w_up, w_down, *, tm):
    P, D = xs.shape
    Fe = w_gate.shape[2]
    nt = P // tm
    hidden = pl.pallas_call(
        _moe_up_kernel,
        out_shape=jax.ShapeDtypeStruct((P, Fe), BF16),
        grid_spec=pltpu.PrefetchScalarGridSpec(
            num_scalar_prefetch=2, grid=(nt,),
            in_specs=[pl.BlockSpec((tm, D), lambda i, te, n: (i, 0)),
                      pl.BlockSpec((None, D, Fe), lambda i, te, n: (te[i], 0, 0)),
                      pl.BlockSpec((None, D, Fe), lambda i, te, n: (te[i], 0, 0)),
                      pl.BlockSpec((tm, 1), lambda i, te, n: (i, 0))],
            out_specs=pl.BlockSpec((tm, Fe), lambda i, te, n: (i, 0))),
        compiler_params=_cparams(("arbitrary",)),
        name="moe_up",
    )(tile_expert, n_tiles, xs, w_gate, w_up, row_w)
    return pl.pallas_call(
        _moe_down_kernel,
        out_shape=jax.ShapeDtypeStruct((P, D), F32),
        grid_spec=pltpu.PrefetchScalarGridSpec(
            num_scalar_prefetch=2, grid=(nt,),
            in_specs=[pl.BlockSpec((tm, Fe), lambda i, te, n: (i, 0)),
                      pl.BlockSpec((None, Fe, D), lambda i, te, n: (te[i], 0, 0))],
            out_specs=pl.BlockSpec((tm, D), lambda i, te, n: (i, 0))),
        compiler_params=_cparams(("arbitrary",)),
        name="moe_down",
    )(tile_expert, n_tiles, hidden, w_down)


def _ffn_ln_kernel(h_ref, f_ref, p_ref, wg_ref, wp_ref, g_ref, b_ref, o_ref, *, alpha):
    h = h_ref[...]
    gate = jax.nn.sigmoid(jnp.dot(h.astype(BF16), wg_ref[...], preferred_element_type=F32))
    ple = gate * jnp.dot(p_ref[...].astype(BF16), wp_ref[...], preferred_element_type=F32)
    o_ref[...] = _layer_norm_rows(alpha * h + f_ref[...] + ple, g_ref[...], b_ref[...])


def ffn_ple_ln(h, ffn, p, w_pg, w_pp, g, b, *, alpha, tm=256):
    M, D = h.shape
    Pd = p.shape[1]
    tm = _pick_tile(M, tm)
    return pl.pallas_call(
        functools.partial(_ffn_ln_kernel, alpha=alpha),
        out_shape=jax.ShapeDtypeStruct((M, D), F32),
        grid=(M // tm,),
        in_specs=[pl.BlockSpec((tm, D), lambda i: (i, 0)),
                  pl.BlockSpec((tm, D), lambda i: (i, 0)),
                  pl.BlockSpec((tm, Pd), lambda i: (i, 0)),
                  pl.BlockSpec((D, D), lambda i: (0, 0)),
                  pl.BlockSpec((Pd, D), lambda i: (0, 0)),
                  pl.BlockSpec((1, D), lambda i: (0, 0)),
                  pl.BlockSpec((1, D), lambda i: (0, 0))],
        out_specs=pl.BlockSpec((tm, D), lambda i: (i, 0)),
        compiler_params=_cparams(("parallel",)),
        name="ffn_ple_ln2",
    )(h, ffn, p, w_pg, w_pp, g, b)


def _alibi_slopes(n):
    return tuple(np.asarray(2.0 ** (-8.0 * np.arange(1, n + 1) / n), dtype=np.float32).tolist())


def _rms_norm(x, g):
    return x * lax.rsqrt(jnp.square(x).mean(-1, keepdims=True) + RMS_EPS) * g


def _rope(x, pos):
    half = x.shape[-1] // 2
    freqs = ROPE_BASE ** (-jnp.arange(half, dtype=F32) / half)
    ang = pos.astype(F32)[:, None] * freqs[None, :]
    shape = (pos.shape[0],) + (1,) * (x.ndim - 3) + (half,)
    cos = jnp.cos(ang).reshape(shape)
    sin = jnp.sin(ang).reshape(shape)
    x1, x2 = x[..., :half], x[..., half:]
    return jnp.concatenate([x1 * cos - x2 * sin, x1 * sin + x2 * cos], -1)


def _pad_axis(x, axis, size):
    if x.shape[axis] == size:
        return x
    pad = [(0, 0)] * x.ndim
    pad[axis] = (0, size - x.shape[axis])
    return jnp.pad(x, pad)


def _topk_mask(score, k):
    n = score.shape[-1]
    mine = score[..., :, None]
    other = score[..., None, :]
    idx = jnp.arange(n)
    beats = (other > mine) | ((other == mine) & (idx[None, :] < idx[:, None]))
    return beats.sum(-1) < k


def _overlap_weights(m, n_cmp, n_sel, nsp):
    cst = np.arange(m) * NSA_CMP_STRIDE
    sst = np.arange(n_sel) * NSA_SEL_BLOCK
    ov = np.clip(np.minimum(cst[:, None] + NSA_CMP_LEN, sst[None, :] + NSA_SEL_BLOCK)
                 - np.maximum(cst[:, None], sst[None, :]), 0, None)
    ovn = np.zeros((m, nsp), np.float32)
    ovn[:n_cmp, :n_sel] = ov[:n_cmp].astype(np.float32) / NSA_CMP_LEN
    return jnp.asarray(ovn)


def _cmp_bias(pe, w1cat):
    out = []
    for kk in range(2):
        pe_part = mm(pe[kk], w1cat[kk], tn=2 * NSA_CMP_HIDDEN, name="nsa_cmp_pos")
        out.append(pe_part[0, :NSA_CMP_HIDDEN] + pe_part[1, NSA_CMP_HIDDEN:])
    return jnp.stack(out)


def moba_attention(mq, k_full, v_full, q_pos0):
    B, T, _ = mq.shape
    H, KVH, HD = MOBA_HEADS, MOBA_KV_HEADS, HEAD_DIM
    G = H // KVH
    L = k_full.shape[1]
    nb = -(-L // MOBA_BLOCK)
    Lp = nb * MOBA_BLOCK
    kf = _pad_axis(k_full, 1, Lp).transpose(0, 2, 1, 3).reshape(B * KVH, Lp, HD)
    vf = _pad_axis(v_full, 1, Lp).transpose(0, 2, 1, 3).reshape(B * KVH, Lp, HD)
    q = mq.reshape(B, T, KVH, G, HD).transpose(0, 2, 1, 3, 4).reshape(B * KVH, T * G, HD)
    k_mean = block_mean(kf, MOBA_BLOCK)
    nbp = _round_up(nb, LANE)
    gate = bmm_f32(q, _pad_axis(k_mean.transpose(0, 2, 1), 2, nbp), name="moba_gate")[..., :nb]
    q_pos = q_pos0 + jnp.arange(T, dtype=jnp.int32)
    own = jnp.repeat(q_pos // MOBA_BLOCK, G)
    past_blk = jnp.arange(nb)[None, :] < own[:, None]
    gate = jnp.where(past_blk[None], gate, NEG_INF)
    picked = (_topk_mask(gate, min(MOBA_TOPK, nb)) & past_blk[None]) | (jnp.arange(nb)[None, :] == own[:, None])[None]
    sel = _pad_axis(picked.astype(F32), 2, nbp)
    out = flash(q, kf, vf, G=G, slopes=_alibi_slopes(H), kv_heads=KVH, qpos0=q_pos0, lk=L, sel=sel,
                sel_blk=MOBA_BLOCK, scale=HD ** -0.5, tr=1024,
                tk=2 * MOBA_BLOCK if Lp % (2 * MOBA_BLOCK) == 0 else MOBA_BLOCK, name="moba_attn")
    return out.reshape(B, KVH, T, G, HD).transpose(0, 2, 1, 3, 4).reshape(B, T, H * HD)


def nsa_attention(nq, ngate, cmp_k, cmp_v, sel_k, sel_v, win_k, win_v, win_pos0, q_pos0, T, cw):
    B = nq.shape[0]
    H, HD = NSA_HEADS, HEAD_DIM
    slopes = _alibi_slopes(H)
    scale = HD ** -0.5
    L = cmp_k.shape[1]
    q = nq.reshape(B, T * H, HD)
    m = L // NSA_CMP_STRIDE
    n_cmp = m - NSA_CMP_LEN // NSA_CMP_STRIDE + 1
    comp = []
    for kk, rows in enumerate((cmp_k, cmp_v)):
        sub = rows[:, :m * NSA_CMP_STRIDE].reshape(B * m, NSA_CMP_STRIDE * HD)
        part = mm(sub, cw['w1cat'][kk], tm=1024, tn=2 * NSA_CMP_HIDDEN, name="nsa_cmp_w1")
        part = part.reshape(B, m, 2 * NSA_CMP_HIDDEN)
        a0 = part[:, :, :NSA_CMP_HIDDEN]
        a1 = jnp.concatenate([part[:, 1:, NSA_CMP_HIDDEN:], jnp.zeros((B, 1, NSA_CMP_HIDDEN), F32)], axis=1)
        c = gelu_proj(a0.reshape(B * m, -1), a1.reshape(B * m, -1), cw['bias'][kk][None, :], cw['w2'][kk])
        comp.append(c.reshape(B, m, HD))
    n_sel = -(-L // NSA_SEL_BLOCK)
    nsp = _round_up(n_sel, LANE)
    o_cmp, imp = cmp_attention(q, comp[0], comp[1], _overlap_weights(m, n_cmp, n_sel, nsp), H=H, slopes=slopes,
                               qpos0=q_pos0, n_cmp=n_cmp, scale=scale, tr=512)
    imp = imp[..., :n_sel]
    q_pos = q_pos0 + jnp.arange(T, dtype=jnp.int32)
    own = q_pos // NSA_SEL_BLOCK
    jj = jnp.arange(n_sel)[None, :]
    forced = (jj == 0) | (jj == own[:, None]) | (jj == own[:, None] - 1)
    valid = jj <= own[:, None]
    score = jnp.where(valid[None], imp + jnp.where(forced[None], FORCE_BONUS, 0.0), NEG_INF)
    sel = _topk_mask(score, min(NSA_TOPN, n_sel)) & valid[None]
    sel = _pad_axis(sel.astype(F32), 2, nsp)
    sel = jnp.repeat(sel, H, axis=1)
    tk = 1024 if L % 1024 == 0 else 512
    Lp = _round_up(L, tk)
    o_sel = flash(q, _pad_axis(sel_k, 1, Lp), _pad_axis(sel_v, 1, Lp), G=H, slopes=slopes, qpos0=q_pos0, lk=L,
                  sel=sel, sel_blk=NSA_SEL_BLOCK, scale=scale, tr=1024, tk=tk, name="nsa_sel_attn")
    Lw = win_k.shape[1]
    tkw = 512 if Lw >= 512 else _round_up(Lw, LANE)
    Lwp = _round_up(Lw, tkw)
    o_win = flash(q, _pad_axis(win_k, 1, Lwp), _pad_axis(win_v, 1, Lwp), G=H, slopes=slopes, qpos0=q_pos0,
                  kpos0=win_pos0, lk=Lw, window=NSA_WINDOW, scale=scale, tr=1024, tk=tkw, name="nsa_win_attn")
    g = jax.nn.sigmoid(ngate.reshape(B, T, 3, H))[..., None]
    o = (g[:, :, 0] * o_cmp.reshape(B, T, H, HD) + g[:, :, 1] * o_sel.reshape(B, T, H, HD)
         + g[:, :, 2] * o_win.reshape(B, T, H, HD))
    return o.reshape(B, T, H * HD)


def moba_attention_paged(mq, proj, past, T):
    B = mq.shape[0]
    H, KVH, HD = MOBA_HEADS, MOBA_KV_HEADS, HEAD_DIM
    G = H // KVH
    q = mq.reshape(B, T, KVH, G, HD).transpose(0, 2, 1, 3, 4).reshape(B, KVH * T * G, HD)
    out = moba_paged(q, proj, past['moba_t'], past['table'], past['layer'], past_len=past['len'], T=T)
    return out.reshape(B, KVH, T, G, HD).transpose(0, 2, 1, 3, 4).reshape(B, T, H * HD)


def nsa_attention_paged(nq, ngate, proj, past, T, cw):
    B = nq.shape[0]
    H, HD = NSA_HEADS, HEAD_DIM
    past_len = past['len']
    L = past_len + T
    m = L // NSA_CMP_STRIDE
    assert m == past_len // NSA_CMP_STRIDE and past_len % NSA_SEL_BLOCK == 0 and T <= NSA_SEL_BLOCK
    n_cmp = m - NSA_CMP_LEN // NSA_CMP_STRIDE + 1
    n_sel = -(-L // NSA_SEL_BLOCK)
    nsp = _round_up(n_sel, LANE)
    q = nq.reshape(B, T * H, HD)
    o_cmp, sel = nsa_cmp_paged(q, cw['w1bd'], cw['bias'].reshape(1, -1), cw['w2'], _overlap_weights(m, n_cmp, n_sel, nsp),
                               past['nsa_t'], past['table'], past['layer'], past_len=past_len, T=T, n_cmp=n_cmp,
                               n_sel=n_sel)
    n_past_blk = past_len // NSA_SEL_BLOCK
    bps = PAGES_PER_STEP * LANE // NSA_SEL_BLOCK
    selx = sel[:, :, :n_past_blk].reshape(B, T, n_past_blk // bps, bps).transpose(0, 2, 1, 3)
    selx = jnp.repeat(selx, H, axis=2)
    selown = jnp.repeat(sel[:, :, n_past_blk], H, axis=1)[..., None]
    o_sel, o_win = nsa_sel_paged(q, selx, selown, proj, past['win_t'], past['nsa_t'], past['table'], past['layer'],
                                 past_len=past_len, T=T)
    g = jax.nn.sigmoid(ngate.reshape(B, T, 3, H))[..., None]
    o = (g[:, :, 0] * o_cmp.reshape(B, T, H, HD) + g[:, :, 1] * o_sel.reshape(B, T, H, HD)
         + g[:, :, 2] * o_win.reshape(B, T, H, HD))
    return o.reshape(B, T, H * HD)


def mla_attention(q_abs, lat_full, q_pos0, T):
    L = lat_full.shape[1]
    tk = 1024 if L % 1024 == 0 else 512
    Lp = _round_up(L, tk)
    kf = _pad_axis(lat_full, 1, Lp)
    return flash(q_abs, kf, kf[..., :MLA_KV_LORA], G=MLA_HEADS, slopes=None, qpos0=q_pos0, lk=L,
                 scale=(MLA_NOPE + MLA_ROPE) ** -0.5, tr=1024, tk=tk, name="mla_attn")


def token_mixer(x2, B, T, q_pos0, past, lw):
    N = B * T
    pos = q_pos0 + jnp.arange(T, dtype=jnp.int32)
    proj = mm(x2, lw['w_in'], tm=1024, tn=512, name="in_proj")
    mq = proj[:, C_MQ:C_MQ + 512].reshape(B, T, 512)
    moba_rows = proj[:, C_MKV:C_MKV + 256].reshape(B, T, MOBA_KV_HEADS, 2, HEAD_DIM)
    nq = proj[:, C_NQ:C_NQ + 512].reshape(B, T, 512)
    nsa_kv = proj[:, C_NKV:C_NKV + 384].reshape(B, T, 6, HEAD_DIM)
    ngate = proj[:, C_NGATE:C_NGATE + 3 * NSA_HEADS]
    qa = proj[:, C_QA:C_QA + MLA_Q_LORA]
    kva = proj[:, C_KVA:C_KVA + MLA_KV_LORA + MLA_ROPE].reshape(B, T, -1)
    nsa_rows, win_rows = nsa_kv[:, :, :4], nsa_kv[:, :, 4:]
    mla_rows = jnp.concatenate([_rms_norm(kva[..., :MLA_KV_LORA], lw['mla_kv_norm']),
                                _rope(kva[..., MLA_KV_LORA:], pos)], -1)
    q_mla = mm(_rms_norm(qa, lw['mla_q_norm']), lw['mla_w_qb'], name="mla_q_up")
    q_mla = q_mla.reshape(B, T, MLA_HEADS, MLA_NOPE + MLA_ROPE)
    q_lat = mm(q_mla[..., :MLA_NOPE].reshape(N, -1), lw['mla_w_uk_bd'], name="mla_q_absorb")
    q_rope = _rope(q_mla[..., MLA_NOPE:], pos)
    q_abs = jnp.concatenate([q_lat.reshape(B, T, MLA_HEADS, MLA_KV_LORA), q_rope], -1)
    q_abs = q_abs.reshape(B, T * MLA_HEADS, MLA_KV_LORA + MLA_ROPE)

    if past is None:
        new_win = win_rows[:, -min(NSA_WINDOW, T):]
        o_a = moba_attention(mq, moba_rows[:, :, :, 0], moba_rows[:, :, :, 1], q_pos0)
        o_b = nsa_attention(nq, ngate, nsa_rows[:, :, 0], nsa_rows[:, :, 1], nsa_rows[:, :, 2], nsa_rows[:, :, 3],
                            win_rows[:, :, 0], win_rows[:, :, 1], 0, q_pos0, T, lw['cmp'])
        o_lat = mla_attention(q_abs, mla_rows, q_pos0, T)
    else:
        new_win = jnp.concatenate([past['win'][:, T:], win_rows], axis=1)
        o_a = moba_attention_paged(mq, proj, past, T)
        o_b = nsa_attention_paged(nq, ngate, proj, past, T, lw['cmp'])
        o_lat = mla_paged(q_abs, mla_rows, past['mla_t'], past['table'], past['layer'], past_len=q_pos0, T=T)
    o_c = mm(o_lat.reshape(N, MLA_HEADS * MLA_KV_LORA), lw['mla_w_uv_bd'], name="mla_v_up")
    d_model = x2.shape[1]
    merged = merge_branches(o_a.reshape(N, -1), o_b.reshape(N, -1), o_c, lw['w_branch'], proj, d_model)
    return merged, moba_rows, nsa_rows, mla_rows, new_win


def moe_ffn(h, router_w_pad, router_bias, lw, *, tm):
    n, D = h.shape
    per_group = N_EXPERTS // N_GROUPS
    logits = mm(h, router_w_pad, precision=HIGHEST, name="router")[:, :N_EXPERTS]
    aff = jax.nn.sigmoid(logits)
    biased = aff + router_bias
    grouped = biased.reshape(n, N_GROUPS, per_group)
    group_score = jnp.where(_topk_mask(grouped, TOPK_EXPERTS), grouped, 0.0).sum(-1)
    in_group = jnp.repeat(_topk_mask(group_score, 1), per_group, axis=1)
    picked = _topk_mask(jnp.where(in_group, biased, NEG_INF), TOPK_EXPERTS)
    nth = jnp.cumsum(picked.astype(jnp.int32), axis=1)
    eidx = jnp.arange(N_EXPERTS, dtype=jnp.int32)[None, :]
    slots = [picked & (nth == k + 1) for k in range(TOPK_EXPERTS)]
    expert_idx = jnp.stack([jnp.where(s_, eidx, 0).sum(-1) for s_ in slots], axis=-1)
    w = jnp.stack([jnp.where(s_, aff, 0.0).sum(-1) for s_ in slots], axis=-1)
    w = w / w.sum(-1, keepdims=True)
    A = n * TOPK_EXPERTS
    e_flat = expert_idx.reshape(A).astype(jnp.int32)
    w_flat = w.reshape(A)
    onehot = (e_flat[:, None] == jnp.arange(N_EXPERTS, dtype=jnp.int32)[None, :]).astype(jnp.int32)
    running = jnp.cumsum(onehot, axis=0)
    rank = jnp.take_along_axis(running, e_flat[:, None], axis=1)[:, 0] - 1
    counts = running[-1]
    padded = -(-counts // tm) * tm
    pend = jnp.cumsum(padded)
    pstart = pend - padded
    dest = pstart[e_flat] + rank
    P = _round_up(A, tm) + N_EXPERTS * tm
    slot_token = jnp.zeros((P,), jnp.int32).at[dest].set(jnp.arange(A, dtype=jnp.int32) // TOPK_EXPERTS)
    slot_w = jnp.zeros((P,), F32).at[dest].set(w_flat)
    pos = dest.reshape(n, TOPK_EXPERTS)
    tile_start = jnp.arange(P // tm, dtype=jnp.int32) * tm
    tile_expert = jnp.minimum((pend[None, :] <= tile_start[:, None]).sum(-1), N_EXPERTS - 1).astype(jnp.int32)
    n_tiles = (pend[-1] // tm).astype(jnp.int32).reshape(1)
    xs = h.astype(BF16)[slot_token]
    y = moe_experts(xs, slot_w[:, None], tile_expert, n_tiles, lw['moe_w_gate'], lw['moe_w_up'], lw['moe_w_down'],
                    tm=tm)
    return y[pos[:, 0]] + y[pos[:, 1]]


def decoder_layer(x2, p2, B, T, q_pos0, past, lw, router_w_pad, router_bias, alpha, moe_tm):
    merged, moba_r, nsa_r, mla_r, win_s = token_mixer(x2, B, T, q_pos0, past, lw)
    h = out_proj_ln(merged, lw['w_out'], x2, lw['ln1_g'], lw['ln1_b'], alpha=alpha)
    ffn = moe_ffn(h, router_w_pad, router_bias, lw, tm=moe_tm)
    y = ffn_ple_ln(h, ffn, p2, lw['ple_w_gate'], lw['ple_w_proj'], lw['ln2_g'], lw['ln2_b'], alpha=alpha)
    return y, (moba_r, nsa_r, mla_r, win_s)


def _pack_w_in(w_in, d_model):
    sizes = (512, 256, 512, 384, 3 * NSA_HEADS, MLA_Q_LORA, MLA_KV_LORA + MLA_ROPE, N_BRANCH * d_model)
    starts = (C_MQ, C_MKV, C_NQ, C_NKV, C_NGATE, C_QA, C_KVA, C_MGATE)
    total = C_MGATE + N_BRANCH * d_model
    total = _round_up(total, 512)
    out = jnp.zeros(w_in.shape[:2] + (total,), BF16)
    off = 0
    for s, c in zip(sizes, starts):
        out = lax.dynamic_update_slice_in_dim(out, w_in[..., off:off + s].astype(BF16), c, axis=2)
        off += s
    return out


def kernel(x_prompt, x_sample, cache_moba, cache_nsa, cache_mla, state_nsa_win, page_table, p_prompt, p_sample,
           w_in, mla_q_norm, mla_w_qb, mla_kv_norm, mla_w_uk, mla_w_uv, nsa_cmp_pos, nsa_cmp_w1, nsa_cmp_w2,
           w_branch, w_out, ln1_g, ln1_b, router_w, router_bias, moe_w_gate, moe_w_up, moe_w_down,
           ple_w_proj, ple_w_gate, ln2_g, ln2_b):
    depth = w_in.shape[0]
    Bp, Tp, D = x_prompt.shape
    Bs, Ts, _ = x_sample.shape
    n_pages, page = page_table.shape[1], cache_moba.shape[2]
    past_len = n_pages * page
    alpha = (2 * depth) ** 0.25
    halves = NSA_CMP_LEN // NSA_CMP_STRIDE

    w_in_p = _pack_w_in(w_in, D)
    eye_h = jnp.eye(MLA_HEADS, dtype=F32)
    w_uk_bd = jnp.einsum('lchn,hg->lhngc', mla_w_uk, eye_h).reshape(depth, MLA_HEADS * MLA_NOPE,
                                                                   MLA_HEADS * MLA_KV_LORA).astype(BF16)
    w_uv_bd = jnp.einsum('lchv,hg->lhcgv', mla_w_uv, eye_h).reshape(depth, MLA_HEADS * MLA_KV_LORA,
                                                                   MLA_HEADS * MLA_V).astype(BF16)
    w1cat = nsa_cmp_w1.reshape(depth, 2, halves, NSA_CMP_STRIDE * HEAD_DIM, NSA_CMP_HIDDEN)
    w1cat = w1cat.transpose(0, 1, 3, 2, 4).reshape(depth, 2, NSA_CMP_STRIDE * HEAD_DIM, halves * NSA_CMP_HIDDEN)
    pe = nsa_cmp_pos.reshape(depth, 2, halves, NSA_CMP_STRIDE * HEAD_DIM)
    pe = jnp.concatenate([pe, jnp.zeros((depth, 2, 8 - halves, NSA_CMP_STRIDE * HEAD_DIM), F32)], axis=2)
    w1r = nsa_cmp_w1.reshape(depth, 2, halves, NSA_CMP_STRIDE, HEAD_DIM, NSA_CMP_HIDDEN).transpose(0, 3, 1, 4, 2, 5)
    w1r = w1r.reshape(depth, NSA_CMP_STRIDE, 2, HEAD_DIM, halves * NSA_CMP_HIDDEN)
    w1bd = jnp.einsum('ljkdc,kq->ljkdqc', w1r, jnp.eye(2, dtype=F32))
    w1bd = w1bd.reshape(depth, NSA_CMP_STRIDE, 2 * HEAD_DIM, 2 * halves * NSA_CMP_HIDDEN).astype(BF16)
    router_w_pad = _pad_axis(router_w, 1, LANE)
    assert page == LANE
    n_pool = cache_moba.shape[1]
    moba_t = cache_moba.transpose(0, 1, 3, 4, 5, 2).reshape(depth, n_pool, MOBA_KV_HEADS * 2 * HEAD_DIM, page)
    nsa_t = cache_nsa.transpose(0, 1, 3, 4, 2).reshape(depth, n_pool, 4 * HEAD_DIM, page)
    mla_t = cache_mla.transpose(0, 1, 3, 2)
    win_t = state_nsa_win.transpose(0, 1, 3, 4, 2).reshape(depth, Bs, 2 * HEAD_DIM, state_nsa_win.shape[2])
    table = page_table.reshape(-1).astype(jnp.int32)
    big = dict(w_branch=w_branch.astype(BF16), w_out=w_out.astype(BF16), moe_w_gate=moe_w_gate.astype(BF16),
               moe_w_up=moe_w_up.astype(BF16), moe_w_down=moe_w_down.astype(BF16),
               ple_w_proj=ple_w_proj.astype(BF16), ple_w_gate=ple_w_gate.astype(BF16),
               mla_w_qb=mla_w_qb.astype(BF16))

    xp = x_prompt.reshape(Bp * Tp, D)
    xs = x_sample.reshape(Bs * Ts, D)
    st_p, st_s = [], []
    for i in range(depth):
        lw = {k: v[i] for k, v in big.items()}
        lw.update(w_in=w_in_p[i], mla_w_uk_bd=w_uk_bd[i], mla_w_uv_bd=w_uv_bd[i],
                  mla_q_norm=mla_q_norm[i], mla_kv_norm=mla_kv_norm[i],
                  cmp=dict(w1cat=w1cat[i], w1bd=w1bd[i], bias=_cmp_bias(pe[i], w1cat[i]), w2=nsa_cmp_w2[i]),
                  ln1_g=ln1_g[i][None], ln1_b=ln1_b[i][None], ln2_g=ln2_g[i][None], ln2_b=ln2_b[i][None])
        past = dict(moba_t=moba_t, nsa_t=nsa_t, mla_t=mla_t, win_t=win_t, win=state_nsa_win[i], table=table,
                    layer=i, len=past_len)
        xp, sp = decoder_layer(xp, p_prompt[i].reshape(Bp * Tp, -1), Bp, Tp, 0, None, lw, router_w_pad, router_bias,
                               alpha, 256)
        xs, ss = decoder_layer(xs, p_sample[i].reshape(Bs * Ts, -1), Bs, Ts, past_len, past, lw, router_w_pad,
                               router_bias, alpha, 128)
        st_p.append(sp)
        st_s.append(ss)
    outs = [xp.reshape(Bp, Tp, D), xs.reshape(Bs, Ts, D)]
    for c in range(4):
        outs.append(jnp.stack([s[c] for s in st_p]))
        outs.append(jnp.stack([s[c] for s in st_s]))
    return tuple(outs)
```
